```python
import math
import jax
import jax.numpy as jnp
from jax import lax
import numpy as np

D_MODEL = 1024
BATCH = 8
SEQ = 4096
DEPTH = 2

GRID_W = 64
BLOCK_Q = 128
EPS = 1e-6
N_BRANCH = 4

A_HEADS = 4
A_DH = 128
A_WIDTH = A_HEADS * A_DH
A_CHUNK = 128
A_CONV = 3
B_HEADS = 8
B_KV = 2
B_DH = 64
B_WIDTH = B_HEADS * B_DH
ROPE_THETA = 10000.0
C_HEADS = 8
C_DH = 64
C_WIDTH = C_HEADS * C_DH
C_WIN_R = 8
C_WIN_C = 16
D_HEADS = 4
D_DH = 64
D_DV = 2 * D_DH
D_WIDTH = D_HEADS * D_DV
D_FF = ((8 * D_MODEL + 3 * 256 - 1) // (3 * 256)) * 256

SPLIT_SIZES = (
    A_WIDTH, A_WIDTH, A_WIDTH, A_WIDTH, 4 * A_HEADS,
    B_WIDTH, B_KV * B_DH, B_KV * B_DH,
    C_WIDTH, C_WIDTH, C_WIDTH,
    2 * D_HEADS * D_DH, 2 * D_HEADS * D_DH, D_WIDTH,
    N_BRANCH * D_MODEL,
)
D_IN = sum(SPLIT_SIZES)

kernel_name = "hybrid_mlstm_gqa_natten_diffattn_encoder"

F32 = jnp.float32


def rms_norm(x, g):
    xf = x.astype(F32)
    y = xf * lax.rsqrt(jnp.mean(xf * xf, axis=-1, keepdims=True) + EPS)
    return (y * g.astype(F32)).astype(x.dtype)


def centred_dwconv(x, w):
    k = w.shape[0]
    p = k // 2
    s = x.shape[1]
    xp = jnp.pad(x, ((0, 0), (p, p), (0, 0)))
    return sum(xp[:, j:j + s, :] * w[j] for j in range(k))


def sweep_query_blocks(fn, q):
    bsz, s = q.shape[:2]
    nb = s // BLOCK_Q
    qb = jnp.moveaxis(q.reshape((bsz, nb, BLOCK_Q) + q.shape[2:]), 1, 0)
    out = lax.map(lambda a: fn(a[0], a[1]), (jnp.arange(nb), qb))
    return jnp.moveaxis(out, 0, 1).reshape((bsz, s) + out.shape[3:])


def mlstm_scan(q, k, v, li, lf):
    bsz, nh, s, d = q.shape
    nc = s // A_CHUNK

    def chunks(t):
        t = t.reshape(t.shape[:2] + (nc, A_CHUNK) + t.shape[3:])
        return jnp.moveaxis(t, 2, 0)

    qc, kc, vc, lic = chunks(q), chunks(k), chunks(v), chunks(li)
    bc = jnp.cumsum(chunks(lf), axis=-1)
    lower = jnp.tril(jnp.ones((A_CHUNK, A_CHUNK), dtype=bool))

    def step(carry, xs):
        c_mat, n_vec, m = carry
        qt, kt, vt, it, bt = xs
        dmat = jnp.where(lower, bt[..., :, None] - bt[..., None, :] + it[..., None, :], -jnp.inf)
        inter = bt + m[..., None]
        mt = jnp.maximum(inter, dmat.max(-1))
        w_inter = jnp.exp(inter - mt)
        sqk = jnp.einsum('bhtd,bhsd->bhts', qt, kt) * jnp.exp(dmat - mt[..., None])
        num = (w_inter[..., None] * jnp.einsum('bhtk,bhkv->bhtv', qt, c_mat)
               + jnp.einsum('bhts,bhsv->bhtv', sqk, vt))
        den = w_inter * jnp.einsum('bhtk,bhk->bht', qt, n_vec) + sqk.sum(-1)
        ht = num / jnp.maximum(jnp.abs(den), jnp.exp(-mt))[..., None]
        bl = bt[..., -1]
        g = bl[..., None] - bt + it
        m_new = jnp.maximum(bl + m, g.max(-1))
        wc = jnp.exp(bl + m - m_new)
        ws = jnp.exp(g - m_new[..., None])
        c_new = wc[..., None, None] * c_mat + jnp.einsum('bhs,bhsk,bhsv->bhkv', ws, kt, vt)
        n_new = wc[..., None] * n_vec + jnp.einsum('bhs,bhsk->bhk', ws, kt)
        return (c_new, n_new, m_new), ht

    init = (jnp.zeros((bsz, nh, d, d), F32), jnp.zeros((bsz, nh, d), F32), jnp.zeros((bsz, nh), F32))
    _, h = lax.scan(step, init, (qc, kc, vc, lic, bc))
    return jnp.moveaxis(h, 0, 2).reshape(bsz, nh, s, d)


def mlstm_branch(q, k, v, o, gates, conv_w, gate_bias, norm_g):
    bsz, s, _ = q.shape
    dt = q.dtype
    qk = jax.nn.silu(centred_dwconv(jnp.concatenate([q, k], axis=-1), conv_w))
    q, k = jnp.split(qk, 2, axis=-1)

    def heads(t):
        return t.reshape(bsz, s, A_HEADS, A_DH).transpose(0, 2, 1, 3).astype(F32)

    qh, kh, vh = heads(q), heads(k) * (A_DH ** -0.5), heads(v)
    g = (gates.astype(F32) + gate_bias.astype(F32)).reshape(bsz, s, 4, A_HEADS).transpose(2, 0, 3, 1)
    i_fwd, f_fwd, i_bwd, f_bwd = g[0], g[1], g[2], g[3]
    h_fwd = mlstm_scan(qh, kh, vh, i_fwd, jax.nn.log_sigmoid(f_fwd))
    flip = lambda t: jnp.flip(t, axis=2)
    h_bwd = flip(mlstm_scan(flip(qh), flip(kh), flip(vh), flip(i_bwd), flip(jax.nn.log_sigmoid(f_bwd))))
    h = h_fwd + h_bwd
    mu = jnp.mean(h, axis=-1, keepdims=True)
    var = jnp.mean(jnp.square(h - mu), axis=-1, keepdims=True)
    h = (h - mu) * lax.rsqrt(var + EPS)
    h = h.transpose(0, 2, 1, 3).reshape(bsz, s, A_WIDTH) * norm_g.astype(F32)
    return (h * jax.nn.sigmoid(o.astype(F32))).astype(dt)


def axial_rope_angles(s):
    t = jnp.arange(s)
    row = (t // GRID_W).astype(F32)
    col = (t % GRID_W).astype(F32)
    n_freq = B_DH // 4
    inv = ROPE_THETA ** (-jnp.arange(n_freq, dtype=F32) / n_freq)
    return row[:, None] * inv, col[:, None] * inv


def rotate_half_rope(x, ang):
    x1, x2 = jnp.split(x, 2, axis=-1)
    c = jnp.cos(ang)[None, :, None, :]
    sn = jnp.sin(ang)[None, :, None, :]
    return jnp.concatenate([x1 * c - x2 * sn, x2 * c + x1 * sn], axis=-1)


def axial_rope(x, ang_r, ang_c):
    xr, xc = jnp.split(x, 2, axis=-1)
    return jnp.concatenate([rotate_half_rope(xr, ang_r), rotate_half_rope(xc, ang_c)], axis=-1)


def gqa_branch(q, k, v, qn_g, kn_g):
    bsz, s, _ = q.shape
    dt = q.dtype
    q = rms_norm(q.reshape(bsz, s, B_HEADS, B_DH), qn_g).astype(F32)
    k = rms_norm(k.reshape(bsz, s, B_KV, B_DH), kn_g).astype(F32)
    v = v.reshape(bsz, s, B_KV, B_DH).astype(F32)
    ang_r, ang_c = axial_rope_angles(s)
    q = axial_rope(q, ang_r, ang_c).reshape(bsz, s, B_KV, B_HEADS // B_KV, B_DH) * (B_DH ** -0.5)
    k = axial_rope(k, ang_r, ang_c)

    def block(_, qb):
        sc = jnp.einsum('bqgrd,bkgd->bgrqk', qb, k)
        p = jax.nn.softmax(sc, axis=-1)
        return jnp.einsum('bgrqk,bkgd->bqgrd', p, v)

    o = sweep_query_blocks(block, q)
    return o.reshape(bsz, s, B_WIDTH).astype(dt)


def natten_indices(s):
    rows = s // GRID_W
    wr = min(C_WIN_R, rows)
    t = jnp.arange(s)
    r = t // GRID_W
    c = t % GRID_W
    rs = jnp.clip(r - wr // 2, 0, rows - wr)
    cs = jnp.clip(c - C_WIN_C // 2, 0, GRID_W - C_WIN_C)
    kr = rs[:, None, None] + jnp.arange(wr)[None, :, None]
    kc = cs[:, None, None] + jnp.arange(C_WIN_C)[None, None, :]
    shape = (s, wr, C_WIN_C)
    idx = jnp.broadcast_to(kr * GRID_W + kc, shape).reshape(s, wr * C_WIN_C)
    off_r = jnp.broadcast_to(kr - r[:, None, None] + (C_WIN_R - 1), shape).reshape(s, wr * C_WIN_C)
    off_c = jnp.broadcast_to(kc - c[:, None, None] + (C_WIN_C - 1), shape).reshape(s, wr * C_WIN_C)
    return idx, off_r, off_c


def natten_branch(q, k, v, rpb):
    bsz, s, _ = q.shape
    dt = q.dtype
    rows = s // GRID_W
    idx, off_r, off_c = natten_indices(s)
    nk = idx.shape[-1]
    bias = rpb.astype(F32)[:, off_r, off_c]
    q = q.reshape(bsz, rows, GRID_W, C_HEADS, C_DH).astype(F32) * (C_DH ** -0.5)
    k = k.reshape(bsz, s, C_HEADS, C_DH).astype(F32)
    v = v.reshape(bsz, s, C_HEADS, C_DH).astype(F32)
    idx_rows = idx.reshape(rows, GRID_W, nk)
    bias_rows = jnp.moveaxis(bias.reshape(C_HEADS, rows, GRID_W, nk), 1, 0)

    def row_block(args):
        qr, ir, br = args
        kg = jnp.take(k, ir, axis=1)
        vg = jnp.take(v, ir, axis=1)
        sc = jnp.einsum('bqhd,bqnhd->bhqn', qr, kg) + br[None]
        p = jax.nn.softmax(sc, axis=-1)
        return jnp.einsum('bhqn,bqnhd->bqhd', p, vg)

    o = lax.map(row_block, (jnp.moveaxis(q, 1, 0), idx_rows, bias_rows))
    return jnp.moveaxis(o, 0, 1).reshape(bsz, s, C_WIDTH).astype(dt)


def diff_branch(q, k, v, lq1, lk1, lq2, lk2, subln_g, lambda_init):
    bsz, s, _ = q.shape
    dt = q.dtype
    q = q.reshape(bsz, s, 2, D_HEADS, D_DH).astype(F32) * (D_DH ** -0.5)
    k = k.reshape(bsz, s, 2, D_HEADS, D_DH).astype(F32)
    v = v.reshape(bsz, s, D_HEADS, D_DV).astype(F32)
    lam = (jnp.exp(jnp.sum(lq1.astype(F32) * lk1.astype(F32)))
           - jnp.exp(jnp.sum(lq2.astype(F32) * lk2.astype(F32))) + lambda_init)
    slopes = 2.0 ** (-8.0 * jnp.arange(1, D_HEADS + 1, dtype=F32) / D_HEADS)
    kpos = jnp.arange(s, dtype=F32)

    def block(i, qb):
        qpos = (i * BLOCK_Q + jnp.arange(BLOCK_Q)).astype(F32)
        alibi = -slopes[:, None, None] * jnp.abs(qpos[:, None] - kpos[None, :])
        sc = jnp.einsum('bqchd,bkchd->bchqk', qb, k) + alibi[None, None]
        p = jax.nn.softmax(sc, axis=-1)
        a = p[:, 0] - lam * p[:, 1]
        return jnp.einsum('bhqk,bkhv->bqhv', a, v)

    o = sweep_query_blocks(block, q)
    o = rms_norm(o, subln_g) * (1.0 - lambda_init)
    return o.reshape(bsz, s, D_WIDTH).astype(dt)


def hybrid_layer(x, layer_idx, norm1_g, w_in, a_conv_w, a_gate_bias, a_norm_g, b_qnorm_g, b_knorm_g,
                 c_rpb, d_lq1, d_lk1, d_lq2, d_lk2, d_subln_g, w_up_a, w_up_b, w_up_c, w_up_d,
                 w_out, norm2_g, w_ffn_gate, w_ffn_up, w_ffn_down):
    bsz, s, _ = x.shape
    dt = x.dtype
    h = rms_norm(x, norm1_g)
    proj = jnp.einsum('bsd,de->bse', h, w_in)
    points = [sum(SPLIT_SIZES[:i + 1]) for i in range(len(SPLIT_SIZES) - 1)]
    (aq, ak, av, ao, ag, bq, bk, bv, cq, ck, cv, dq, dk, dv, gl) = jnp.split(proj, points, axis=-1)
    y_a = mlstm_branch(aq, ak, av, ao, ag, a_conv_w, a_gate_bias, a_norm_g)
    y_b = gqa_branch(bq, bk, bv, b_qnorm_g, b_knorm_g)
    y_c = natten_branch(cq, ck, cv, c_rpb)
    lambda_init = 0.8 - 0.6 * math.exp(-0.3 * layer_idx)
    y_d = diff_branch(dq, dk, dv, d_lq1, d_lk1, d_lq2, d_lk2, d_subln_g, lambda_init)
    g = jax.nn.sigmoid(gl.astype(F32)).reshape(bsz, s, N_BRANCH, D_MODEL)
    merged = (g[:, :, 0] * (y_a @ w_up_a) + g[:, :, 1] * (y_b @ w_up_b)
              + g[:, :, 2] * (y_c @ w_up_c) + g[:, :, 3] * (y_d @ w_up_d))
    x = x + merged.astype(dt) @ w_out
    h2 = rms_norm(x, norm2_g)
    ffn = (jax.nn.silu(h2 @ w_ffn_gate) * (h2 @ w_ffn_up)) @ w_ffn_down
    return x + ffn


def setup_inputs(seed: int = 0) -> dict:
    key = jax.random.key(seed)
    ks = jax.random.split(key, 26)
    L = DEPTH
    nrm = lambda k, shape, scale: jax.random.normal(k, shape, F32) * scale
    gain = lambda k, shape: 1.0 + 0.02 * jax.random.normal(k, shape, F32)
    fb = jnp.linspace(3.0, 6.0, A_HEADS, dtype=F32)
    zh = jnp.zeros((A_HEADS,), F32)
    gate_offset = jnp.concatenate([zh, fb, zh, fb])
    return {
        "x": jax.random.normal(ks[0], (BATCH, SEQ, D_MODEL), F32),
        "norm1_g": gain(ks[1], (L, D_MODEL)),
        "w_in": nrm(ks[2], (L, D_MODEL, D_IN), D_MODEL ** -0.5),
        "a_conv_w": nrm(ks[3], (L, A_CONV, 2 * A_WIDTH), A_CONV ** -0.5),
        "a_gate_bias": gate_offset[None] + nrm(ks[4], (L, 4 * A_HEADS), 0.1),
        "a_norm_g": gain(ks[5], (L, A_WIDTH)),
        "b_qnorm_g": gain(ks[6], (L, B_DH)),
        "b_knorm_g": gain(ks[7], (L, B_DH)),
        "c_rpb": nrm(ks[8], (L, C_HEADS, 2 * C_WIN_R - 1, 2 * C_WIN_C - 1), 0.02),
        "d_lambda_q1": nrm(ks[9], (L, D_DH), 0.1),
        "d_lambda_k1": nrm(ks[10], (L, D_DH), 0.1),
        "d_lambda_q2": nrm(ks[11], (L, D_DH), 0.1),
        "d_lambda_k2": nrm(ks[12], (L, D_DH), 0.1),
        "d_subln_g": gain(ks[13], (L, D_DV)),
        "w_up_a": nrm(ks[14], (L, A_WIDTH, D_MODEL), A_WIDTH ** -0.5),
        "w_up_b": nrm(ks[15], (L, B_WIDTH, D_MODEL), B_WIDTH ** -0.5),
        "w_up_c": nrm(ks[16], (L, C_WIDTH, D_MODEL), C_WIDTH ** -0.5),
        "w_up_d": nrm(ks[17], (L, D_WIDTH, D_MODEL), D_WIDTH ** -0.5),
        "w_out": nrm(ks[18], (L, D_MODEL, D_MODEL), D_MODEL ** -0.5),
        "norm2_g": gain(ks[19], (L, D_MODEL)),
        "w_ffn_gate": nrm(ks[20], (L, D_MODEL, D_FF), D_MODEL ** -0.5),
        "w_ffn_up": nrm(ks[21], (L, D_MODEL, D_FF), D_MODEL ** -0.5),
        "w_ffn_down": nrm(ks[22], (L, D_FF, D_MODEL), D_FF ** -0.5),
        "final_g": gain(ks[23], (D_MODEL,)),
    }


def reference(x, norm1_g, w_in, a_conv_w, a_gate_bias, a_norm_g, b_qnorm_g, b_knorm_g, c_rpb,
              d_lambda_q1, d_lambda_k1, d_lambda_q2, d_lambda_k2, d_subln_g,
              w_up_a, w_up_b, w_up_c, w_up_d, w_out, norm2_g, w_ffn_gate, w_ffn_up, w_ffn_down, final_g):
    for l in range(DEPTH):
        x = hybrid_layer(x, l, norm1_g[l], w_in[l], a_conv_w[l], a_gate_bias[l], a_norm_g[l],
                         b_qnorm_g[l], b_knorm_g[l], c_rpb[l],
                         d_lambda_q1[l], d_lambda_k1[l], d_lambda_q2[l], d_lambda_k2[l], d_subln_g[l],
                         w_up_a[l], w_up_b[l], w_up_c[l], w_up_d[l], w_out[l],
                         norm2_g[l], w_ffn_gate[l], w_ffn_up[l], w_ffn_down[l])
    return rms_norm(x, final_g)
```

```python
import functools
import math

import jax
import jax.numpy as jnp
from jax import lax
from jax.experimental import pallas as pl
from jax.experimental.pallas import tpu as pltpu

F32 = jnp.float32
BF16 = jnp.bfloat16

D_MODEL = 1024
DEPTH = 2
GRID_W = 64
EPS = 1e-6
N_BRANCH = 4
A_HEADS, A_DH, A_CHUNK = 4, 128, 128
A_WIDTH = A_HEADS * A_DH
B_HEADS, B_KV, B_DH = 8, 2, 64
B_WIDTH = B_HEADS * B_DH
ROPE_THETA = 10000.0
C_HEADS, C_DH, C_WIN_R, C_WIN_C = 8, 64, 8, 16
C_WIDTH = C_HEADS * C_DH
D_HEADS, D_DH = 4, 64
D_DV = 2 * D_DH
D_WIDTH = D_HEADS * D_DV
D_FF = ((8 * D_MODEL + 3 * 256 - 1) // (3 * 256)) * 256
FF_SPLITS = ((0, 1536), (1536, D_FF))

_SIZES = (A_WIDTH, A_WIDTH, A_WIDTH, A_WIDTH, 4 * A_HEADS, B_WIDTH, B_KV * B_DH, B_KV * B_DH,
          C_WIDTH, C_WIDTH, C_WIDTH, 2 * D_HEADS * D_DH, 2 * D_HEADS * D_DH, D_WIDTH, N_BRANCH * D_MODEL)
_NAMES = ("aq", "ak", "av", "ao", "ag", "bq", "bk", "bv", "cq", "ck", "cv", "dq", "dk", "dv", "gl")
_OFF = {}
_o = 0
for _n, _s in zip(_NAMES, _SIZES):
    _OFF[_n] = (_o, _o + _s)
    _o += _s

_MAIN_ORDER = ("gl", "aq", "ak", "av", "ao", "cq", "ck", "cv", "dk")
_MAIN_OFF = {}
_o = 0
for _n in _MAIN_ORDER:
    _MAIN_OFF[_n] = _o
    _o += _OFF[_n][1] - _OFF[_n][0]
MAIN_N = _o

KC = 512
VMEM_LIMIT = 56 * 1024 * 1024


def _cparams(sem, vmem=VMEM_LIMIT):
    return pltpu.CompilerParams(dimension_semantics=sem, vmem_limit_bytes=vmem)


def _sigmoid(x):
    return 1.0 / (1.0 + jnp.exp(-x))


def _proj_kernel(x_ref, g_ref, w_ref, wg_ref, o_ref, og_ref, h_ref):
    @pl.when(pl.program_id(1) == 0)
    def _():
        x = x_ref[...]
        ms = jnp.mean(x * x, axis=-1, keepdims=True)
        hb = (x * lax.rsqrt(ms + EPS) * g_ref[...]).astype(BF16)
        h_ref[...] = hb
        og_ref[...] = jnp.dot(hb, wg_ref[...], preferred_element_type=F32)

    o_ref[...] = jnp.dot(h_ref[...], w_ref[...], preferred_element_type=F32).astype(o_ref.dtype)


def _proj(x2, g, w_main, w_gate):
    t = x2.shape[0]
    tm = min(1024, t)
    tn = 1024
    n = w_main.shape[1]
    ng = w_gate.shape[1]
    return pl.pallas_call(
        _proj_kernel,
        out_shape=(jax.ShapeDtypeStruct((t, n), BF16),
                   jax.ShapeDtypeStruct((t, ng), F32),
                   jax.ShapeDtypeStruct((t, D_MODEL), BF16)),
        grid=(t // tm, n // tn),
        in_specs=[pl.BlockSpec((tm, D_MODEL), lambda i, j: (i, 0)),
                  pl.BlockSpec((1, D_MODEL), lambda i, j: (0, 0)),
                  pl.BlockSpec((D_MODEL, tn), lambda i, j: (0, j)),
                  pl.BlockSpec((D_MODEL, ng), lambda i, j: (0, 0))],
        out_specs=(pl.BlockSpec((tm, tn), lambda i, j: (i, j)),
                   pl.BlockSpec((tm, ng), lambda i, j: (i, 0)),
                   pl.BlockSpec((tm, D_MODEL), lambda i, j: (i, 0))),
        compiler_params=_cparams(("parallel", "arbitrary")),
        name="proj_main",
    )(x2, g, w_main, w_gate)


def _projt_kernel(h_ref, wt_ref, o_ref, *, nchunk):
    for c in range(nchunk):
        hc = h_ref[c * KC:(c + 1) * KC, :]
        o_ref[c] = lax.dot_general(wt_ref[...], hc, (((1,), (1,)), ((), ())),
                                   preferred_element_type=F32).astype(o_ref.dtype)


def _projt(h, wt, out_dtype, name):
    t = h.shape[0]
    n = wt.shape[0]
    tm = min(2048, t)
    tn = min(768, n) if n % 768 == 0 else 512
    nchunk = tm // KC
    return pl.pallas_call(
        functools.partial(_projt_kernel, nchunk=nchunk),
        out_shape=jax.ShapeDtypeStruct((t // KC, n, KC), out_dtype),
        grid=(t // tm, n // tn),
        in_specs=[pl.BlockSpec((tm, D_MODEL), lambda i, j: (i, 0)),
                  pl.BlockSpec((tn, D_MODEL), lambda i, j: (j, 0))],
        out_specs=pl.BlockSpec((nchunk, tn, KC), lambda i, j: (i, j, 0)),
        compiler_params=_cparams(("parallel", "arbitrary")),
        name=name,
    )(h, wt)


def _log_sigmoid(x):
    return jnp.minimum(x, 0.0) - jnp.log1p(jnp.exp(-jnp.abs(x)))


def _tri_dot(mat, x):
    hi = x.astype(BF16)
    r1 = x - hi.astype(F32)
    mid = r1.astype(BF16)
    lo = (r1 - mid.astype(F32)).astype(BF16)
    return (jnp.dot(mat, hi, preferred_element_type=F32)
            + jnp.dot(mat, mid, preferred_element_type=F32)
            + jnp.dot(mat, lo, preferred_element_type=F32))


def _mlstm_kernel(q_ref, k_ref, v_ref, o_ref, g_ref, gb_ref, cw_ref, ng_ref, y_ref,
                  qs_ref, ks_ref, hs_ref, c_ref, n_ref, m_ref, *, seq):
    L = A_CHUNK
    nc = seq // L
    hw = 2 * A_DH

    rowi = lax.broadcasted_iota(jnp.int32, (L, hw), 0)

    def conv_body(c, carry):
        r0 = pl.multiple_of(c * L, L)
        pstart = pl.multiple_of(jnp.maximum(r0 - 16, 0), 16)
        nstart = pl.multiple_of(jnp.minimum(r0 + L, seq - 16), 16)
        has_prev = jnp.where(c > 0, 1.0, 0.0)
        has_next = jnp.where(c < nc - 1, 1.0, 0.0)
        for src, dst, woff, scale in ((q_ref, qs_ref, 0, 1.0), (k_ref, ks_ref, hw, A_DH ** -0.5)):
            xc = src[pl.ds(r0, L), :].astype(F32)
            prev = src[pl.ds(pstart, 16), :].astype(F32)[15:16] * has_prev
            nxt = src[pl.ds(nstart, 16), :].astype(F32)[0:1] * has_next
            xp = jnp.where(rowi == 0, prev, pltpu.roll(xc, 1, 0))
            xn = jnp.where(rowi == L - 1, nxt, pltpu.roll(xc, L - 1, 0))
            w = cw_ref[:, woff:woff + hw]
            y = xp * w[0:1] + xc * w[1:2] + xn * w[2:3]
            y = y * _sigmoid(y)
            dst[pl.ds(r0, L), :] = (y * scale).astype(BF16)
        return carry

    lax.fori_loop(0, nc, conv_body, 0)

    c_ref[...] = jnp.zeros(c_ref.shape, F32)
    n_ref[...] = jnp.zeros(n_ref.shape, F32)
    m_ref[...] = jnp.zeros(m_ref.shape, F32)

    ti = lax.broadcasted_iota(jnp.int32, (L, L), 0)
    tj = lax.broadcasted_iota(jnp.int32, (L, L), 1)
    lower = tj <= ti
    upper = tj >= ti
    lmat = jnp.where(lower, 1.0, 0.0).astype(BF16)
    umat = jnp.where(upper, 1.0, 0.0).astype(BF16)

    def direction(c, d):
        r0 = pl.multiple_of(c * L, L)
        gcol = g_ref[pl.ds(r0, L), :] + gb_ref[...]
        bcol_all = _tri_dot(lmat if d == 0 else umat, _log_sigmoid(gcol))
        grow = gcol.T
        brow_all = bcol_all.T
        mask = lower if d == 0 else upper
        outs = []
        for hh in range(2):
            il = 4 * d + hh
            fl = il + 2
            idx = 2 * d + hh
            icol = gcol[:, il:il + 1]
            bcol = bcol_all[:, fl:fl + 1]
            irow = grow[il:il + 1, :]
            brow = brow_all[fl:fl + 1, :]
            m_prev = m_ref[idx][:, 0:1]
            dmat = jnp.where(mask, bcol - brow + irow, -jnp.inf)
            inter = bcol + m_prev
            mt = jnp.maximum(inter, jnp.max(dmat, axis=1, keepdims=True))
            w_inter = jnp.exp(inter - mt)
            q = qs_ref[pl.ds(r0, L), hh * A_DH:(hh + 1) * A_DH]
            k = ks_ref[pl.ds(r0, L), hh * A_DH:(hh + 1) * A_DH]
            v = v_ref[pl.ds(r0, L), hh * A_DH:(hh + 1) * A_DH]
            qk = lax.dot_general(q, k, (((1,), (1,)), ((), ())), preferred_element_type=F32)
            sqk = qk * jnp.exp(dmat - mt)
            cmat = c_ref[idx]
            nrow = n_ref[idx]
            num = (w_inter * jnp.dot(q, cmat.astype(BF16), preferred_element_type=F32)
                   + jnp.dot(sqk.astype(BF16), v, preferred_element_type=F32))
            den = (w_inter * jnp.sum(q.astype(F32) * nrow, axis=1, keepdims=True)
                   + jnp.sum(sqk, axis=1, keepdims=True))
            outs.append(num / jnp.maximum(jnp.abs(den), jnp.exp(-mt)))
            bl = bcol[L - 1:L] if d == 0 else bcol[0:1]
            gvec = bl - bcol + icol
            m_new = jnp.maximum(bl + m_prev, jnp.max(gvec, axis=0, keepdims=True))
            wc = jnp.exp(bl + m_prev - m_new)
            kw = k.astype(F32) * jnp.exp(gvec - m_new)
            c_ref[idx] = wc * cmat + jnp.dot(kw.T.astype(BF16), v, preferred_element_type=F32)
            n_ref[idx] = wc * nrow + jnp.sum(kw, axis=0, keepdims=True)
            m_ref[idx] = jnp.broadcast_to(m_new, (1, 128))
        return jnp.concatenate(outs, axis=1)

    def finalize(tot, r0):
        parts = []
        for hh in range(2):
            hcur = tot[:, hh * A_DH:(hh + 1) * A_DH]
            mu = jnp.mean(hcur, axis=1, keepdims=True)
            cen = hcur - mu
            var = jnp.mean(cen * cen, axis=1, keepdims=True)
            parts.append(cen * lax.rsqrt(var + EPS))
        hn = jnp.concatenate(parts, axis=1) * ng_ref[...]
        y_ref[pl.ds(r0, L), :] = (hn * _sigmoid(o_ref[pl.ds(r0, L), :].astype(F32))).astype(y_ref.dtype)

    def first_half(j, carry):
        for d, c in ((0, j), (1, nc - 1 - j)):
            hs_ref[pl.ds(pl.multiple_of(c * L, L), L), :] = direction(c, d)
        return carry

    def second_half(j, carry):
        for d, c in ((0, j), (1, nc - 1 - j)):
            r0 = pl.multiple_of(c * L, L)
            finalize(hs_ref[pl.ds(r0, L), :] + direction(c, d), r0)
        return carry

    lax.fori_loop(0, nc // 2, first_half, 0)
    lax.fori_loop(nc // 2, nc, second_half, 0)


def _mlstm(main, gates, gate_bias, conv_w, norm_g, bsz, seq):
    t = bsz * seq
    hw = 2 * A_DH
    assert (seq // A_CHUNK) % 2 == 0

    def col(name):
        base = _MAIN_OFF[name] // hw
        return pl.BlockSpec((seq, hw), lambda b, p: (b, base + p))

    return pl.pallas_call(
        functools.partial(_mlstm_kernel, seq=seq),
        out_shape=jax.ShapeDtypeStruct((t, A_WIDTH), BF16),
        grid=(bsz, 2),
        in_specs=[col("aq"), col("ak"), col("av"), col("ao"),
                  pl.BlockSpec((seq, 128), lambda b, p: (b, p)),
                  pl.BlockSpec((1, 128), lambda b, p: (0, p)),
                  pl.BlockSpec((None, 3, 2 * hw), lambda b, p: (p, 0, 0)),
                  pl.BlockSpec((1, hw), lambda b, p: (0, p))],
        out_specs=pl.BlockSpec((seq, hw), lambda b, p: (b, p)),
        scratch_shapes=[pltpu.VMEM((seq, hw), BF16), pltpu.VMEM((seq, hw), BF16),
                        pltpu.VMEM((seq, hw), F32),
                        pltpu.VMEM((4, A_DH, A_DH), F32), pltpu.VMEM((4, 1, A_DH), F32),
                        pltpu.VMEM((4, 1, 128), F32)],
        compiler_params=_cparams(("parallel", "arbitrary")),
        name="mlstm",
    )(main, main, main, main, gates, gate_bias, conv_w, norm_g)


def _flash_update(s, vc, m_ref, l_ref, acc_ref, idx):
    m_old = m_ref[idx]
    m_new = jnp.maximum(m_old, jnp.max(s, axis=0, keepdims=True))
    alpha = jnp.exp(m_old - m_new)
    p = jnp.exp(s - m_new)
    l_ref[idx] = alpha * l_ref[idx] + jnp.sum(p, axis=0, keepdims=True)
    acc_ref[idx] = alpha * acc_ref[idx] + jnp.dot(vc, p.astype(BF16), preferred_element_type=F32)
    m_ref[idx] = m_new


def _gqa_prep_kernel(fm_ref, cos_ref, sin_ref, qg_ref, kg_ref, q_ref, k_ref, v_ref):
    x = fm_ref[0]
    cos = cos_ref[...]
    sin = sin_ref[...]

    def norm_rope(xh, g):
        ms = jnp.mean(xh * xh, axis=0, keepdims=True)
        xn = xh * lax.rsqrt(ms + EPS) * g
        partner = jnp.concatenate([xn[16:32], xn[0:16], xn[48:64], xn[32:48]], axis=0)
        return xn * cos + partner * sin

    for h in range(B_HEADS):
        qh = norm_rope(x[h * B_DH:(h + 1) * B_DH], qg_ref[...]) * (B_DH ** -0.5)
        q_ref[0, h * B_DH:(h + 1) * B_DH, :] = qh.astype(BF16)
    ks = [norm_rope(x[B_WIDTH + g * B_DH:B_WIDTH + (g + 1) * B_DH], kg_ref[...]) for g in range(B_KV)]
    k_ref[...] = jnp.concatenate(ks, axis=0).T.astype(BF16)
    v_ref[0] = x[B_WIDTH + B_KV * B_DH:].astype(BF16)


def _gqa_prep(fm_b, cos, sin, qg, kg, seq):
    nchunks = fm_b.shape[0]
    t = nchunks * KC
    spc = seq // KC
    kvw = B_KV * B_DH
    return pl.pallas_call(
        _gqa_prep_kernel,
        out_shape=(jax.ShapeDtypeStruct((nchunks, B_WIDTH, KC), BF16),
                   jax.ShapeDtypeStruct((t, kvw), BF16),
                   jax.ShapeDtypeStruct((nchunks, kvw, KC), BF16)),
        grid=(nchunks,),
        in_specs=[pl.BlockSpec((1, B_WIDTH + 2 * kvw, KC), lambda i: (i, 0, 0)),
                  pl.BlockSpec((B_DH, KC), lambda i: (0, i % spc)),
                  pl.BlockSpec((B_DH, KC), lambda i: (0, i % spc)),
                  pl.BlockSpec((B_DH, 1), lambda i: (0, 0)),
                  pl.BlockSpec((B_DH, 1), lambda i: (0, 0))],
        out_specs=(pl.BlockSpec((1, B_WIDTH, KC), lambda i: (i, 0, 0)),
                   pl.BlockSpec((KC, kvw), lambda i: (i, 0)),
                   pl.BlockSpec((1, kvw, KC), lambda i: (i, 0, 0))),
        compiler_params=_cparams(("parallel",)),
        name="gqa_prep",
    )(fm_b, cos, sin, qg, kg)


def _gqa_attn_kernel(q_ref, k_ref, v_ref, o_ref, qpad_ref, m_ref, l_ref, acc_ref, *, nkc, tq):
    g = pl.program_id(1)
    rep = B_HEADS // B_KV
    row = lax.broadcasted_iota(jnp.int32, (2 * B_DH, tq), 0)
    sel = (row // B_DH) == g
    for r in range(rep):
        qh = q_ref[0, r * B_DH:(r + 1) * B_DH, :].astype(F32)
        qpad_ref[r] = jnp.where(sel, jnp.concatenate([qh, qh], axis=0), 0.0).astype(BF16)
    m_ref[...] = jnp.full(m_ref.shape, -jnp.inf, F32)
    l_ref[...] = jnp.zeros(l_ref.shape, F32)
    acc_ref[...] = jnp.zeros(acc_ref.shape, F32)

    def body(c, carry):
        kc = k_ref[pl.ds(pl.multiple_of(c * KC, KC), KC), :]
        vc = v_ref[c]
        for r in range(rep):
            s = jnp.dot(kc, qpad_ref[r], preferred_element_type=F32)
            _flash_update(s, vc, m_ref, l_ref, acc_ref, r)
        return carry

    lax.fori_loop(0, nkc, body, 0)
    outs = [acc_ref[r] / l_ref[r] for r in range(rep)]
    o_ref[...] = jnp.concatenate(outs, axis=0).T.astype(o_ref.dtype)


def _gqa_attn(qt, k_tok, v_fm, bsz, seq):
    t = bsz * seq
    tq = min(256, seq)
    nq = seq // tq
    nkc = seq // KC
    qpc = KC // tq
    rep = B_HEADS // B_KV
    gw = rep * B_DH
    return pl.pallas_call(
        functools.partial(_gqa_attn_kernel, nkc=nkc, tq=tq),
        out_shape=jax.ShapeDtypeStruct((t, B_WIDTH), BF16),
        grid=(bsz, B_KV, nq),
        in_specs=[pl.BlockSpec((1, gw, tq), lambda b, g, i: (b * nkc + i // qpc, g, i % qpc)),
                  pl.BlockSpec((seq, B_KV * B_DH), lambda b, g, i: (b, 0)),
                  pl.BlockSpec((nkc, B_DH, KC), lambda b, g, i: (b, g, 0))],
        out_specs=pl.BlockSpec((tq, gw), lambda b, g, i: (b * nq + i, g)),
        scratch_shapes=[pltpu.VMEM((rep, 2 * B_DH, tq), BF16),
                        pltpu.VMEM((rep, 1, tq), F32), pltpu.VMEM((rep, 1, tq), F32),
                        pltpu.VMEM((rep, B_DH, tq), F32)],
        compiler_params=_cparams(("parallel", "parallel", "arbitrary")),
        name="gqa_attn",
    )(qt, k_tok, v_fm)


def _natten_kernel(q_ref, k_ref, v_ref, bias_ref, o_ref, *, rows):
    r = pl.program_id(1)
    rs = jnp.clip(r - C_WIN_R // 2, 0, rows - C_WIN_R)
    start = pl.multiple_of(rs * GRID_W, GRID_W)
    nkeys = C_WIN_R * GRID_W
    lane = lax.broadcasted_iota(jnp.int32, (GRID_W, 2 * C_DH), 1)
    outs = []
    for p in range(C_HEADS // 2):
        kp = k_ref[pl.ds(start, nkeys), p * 128:(p + 1) * 128]
        vp = v_ref[pl.ds(start, nkeys), p * 128:(p + 1) * 128]
        qp = q_ref[:, p * 128:(p + 1) * 128].astype(F32)
        op = jnp.zeros((GRID_W, 2 * C_DH), F32)
        for hh in range(2):
            selh = (lane // C_DH) == hh
            qm = jnp.where(selh, qp, 0.0).astype(BF16)
            s = lax.dot_general(qm, kp, (((1,), (1,)), ((), ())), preferred_element_type=F32)
            s = s + bias_ref[0, 2 * p + hh]
            e = jnp.exp(s - jnp.max(s, axis=1, keepdims=True))
            o = jnp.dot(e.astype(BF16), vp, preferred_element_type=F32) / jnp.sum(e, axis=1, keepdims=True)
            op = jnp.where(selh, o, op)
        outs.append(op)
    o_ref[...] = jnp.concatenate(outs, axis=1).astype(o_ref.dtype)


def _natten_bias_table(rpb, rows):
    v = jnp.arange(C_WIN_R)[:, None, None, None]
    c = jnp.arange(GRID_W)[None, :, None, None]
    kro = jnp.arange(C_WIN_R)[None, None, :, None]
    kc = jnp.arange(GRID_W)[None, None, None, :]
    cs = jnp.clip(c - C_WIN_C // 2, 0, GRID_W - C_WIN_C)
    valid = (kc >= cs) & (kc < cs + C_WIN_C)
    off_r = jnp.broadcast_to(kro - v + (C_WIN_R - 1), (C_WIN_R, GRID_W, C_WIN_R, GRID_W))
    off_c = jnp.clip(kc - c + (C_WIN_C - 1), 0, 2 * C_WIN_C - 2)
    off_c = jnp.broadcast_to(off_c, off_r.shape)
    tbl = rpb.astype(F32)[:, off_r, off_c]
    tbl = jnp.where(valid[None], tbl, -1e30)
    return jnp.moveaxis(tbl, 0, 1).reshape(C_WIN_R, C_HEADS, GRID_W, C_WIN_R * GRID_W)


def _natten(main, bias_tbl, bsz, seq):
    t = bsz * seq
    rows = seq // GRID_W
    assert rows >= C_WIN_R
    half = C_WIN_R // 2

    def col(name):
        return _MAIN_OFF[name] // C_WIDTH

    cq, ck, cv = col("cq"), col("ck"), col("cv")
    return pl.pallas_call(
        functools.partial(_natten_kernel, rows=rows),
        out_shape=jax.ShapeDtypeStruct((t, C_WIDTH), BF16),
        grid=(bsz, rows),
        in_specs=[pl.BlockSpec((GRID_W, C_WIDTH), lambda b, r: (b * rows + r, cq)),
                  pl.BlockSpec((seq, C_WIDTH), lambda b, r: (b, ck)),
                  pl.BlockSpec((seq, C_WIDTH), lambda b, r: (b, cv)),
                  pl.BlockSpec((1, C_HEADS, GRID_W, C_WIN_R * GRID_W),
                               lambda b, r: (r - jnp.clip(r - half, 0, rows - C_WIN_R), 0, 0, 0))],
        out_specs=pl.BlockSpec((GRID_W, C_WIDTH), lambda b, r: (b * rows + r, 0)),
        compiler_params=_cparams(("parallel", "arbitrary")),
        name="natten",
    )(main, main, main, bias_tbl)


def _diff_attn_kernel(slopes_ref, q1_ref, q2_ref, k1_ref, k2_ref, v_ref, lq1_ref, lk1_ref, lq2_ref, lk2_ref,
                      sg_ref, o_ref, qpad_ref, e_ref, m_ref, l_ref, acc_ref, *, nkc, tq, lambda_init):
    pair = pl.program_id(1)
    qi = pl.program_id(2)
    row = lax.broadcasted_iota(jnp.int32, (2 * D_DH, tq), 0)
    for c, qr in enumerate((q1_ref, q2_ref)):
        qf = qr[0].astype(F32)
        for hh in range(2):
            qpad_ref[2 * c + hh] = jnp.where((row // D_DH) == hh, qf, 0.0).astype(BF16)
    m_ref[...] = jnp.full(m_ref.shape, -jnp.inf, F32)
    l_ref[...] = jnp.zeros(l_ref.shape, F32)
    acc_ref[...] = jnp.zeros(acc_ref.shape, F32)
    kk = lax.broadcasted_iota(jnp.int32, (KC, tq), 0)
    qq = lax.broadcasted_iota(jnp.int32, (KC, tq), 1)
    e_ref[...] = (kk - qq).astype(F32)

    def body(ci, carry):
        off = (ci * KC - qi * tq).astype(F32)
        dist = jnp.abs(e_ref[...] + off)
        r0 = pl.multiple_of(ci * KC, KC)
        for hh in range(2):
            bias = dist * (-slopes_ref[2 * pair + hh])
            vc = v_ref[ci, hh * D_DV:(hh + 1) * D_DV, :]
            for c, kr in enumerate((k1_ref, k2_ref)):
                s = jnp.dot(kr[pl.ds(r0, KC), :], qpad_ref[2 * c + hh], preferred_element_type=F32) + bias
                _flash_update(s, vc, m_ref, l_ref, acc_ref, 2 * c + hh)
        return carry

    lax.fori_loop(0, nkc, body, 0)
    lam = (jnp.exp(jnp.sum(lq1_ref[...] * lk1_ref[...], axis=1, keepdims=True))
           - jnp.exp(jnp.sum(lq2_ref[...] * lk2_ref[...], axis=1, keepdims=True)) + lambda_init)
    outs = []
    for hh in range(2):
        o = acc_ref[hh] / l_ref[hh] - lam * (acc_ref[2 + hh] / l_ref[2 + hh])
        ms = jnp.mean(o * o, axis=0, keepdims=True)
        outs.append(o * lax.rsqrt(ms + EPS) * sg_ref[...] * (1.0 - lambda_init))
    o_ref[...] = jnp.concatenate(outs, axis=0).T.astype(o_ref.dtype)


def _diff_attn(fm_d, main, lq1, lk1, lq2, lk2, subln_g, lambda_init, bsz, seq):
    t = bsz * seq
    tq = min(256, seq)
    nq = seq // tq
    nkc = seq // KC
    qpc = KC // tq
    pw = 2 * D_DH
    vw = 2 * D_DV
    dk0 = _MAIN_OFF["dk"] // pw
    vrow0 = (2 * D_HEADS * D_DH) // vw
    slopes = jnp.asarray([2.0 ** (-8.0 * (h + 1) / D_HEADS) for h in range(D_HEADS)], F32)

    def qspec(c):
        return pl.BlockSpec((1, pw, tq), lambda b, p, i: (b * nkc + i // qpc, 2 * c + p, i % qpc))

    def kspec(c):
        return pl.BlockSpec((seq, pw), lambda b, p, i: (b, dk0 + 2 * c + p))

    vec = pl.BlockSpec((1, D_DH), lambda b, p, i: (0, 0))
    return pl.pallas_call(
        functools.partial(_diff_attn_kernel, nkc=nkc, tq=tq, lambda_init=lambda_init),
        out_shape=jax.ShapeDtypeStruct((t, D_WIDTH), BF16),
        grid=(bsz, D_HEADS // 2, nq),
        in_specs=[pl.BlockSpec(memory_space=pltpu.SMEM),
                  qspec(0), qspec(1), kspec(0), kspec(1),
                  pl.BlockSpec((nkc, vw, KC), lambda b, p, i: (b, vrow0 + p, 0)),
                  vec, vec, vec, vec,
                  pl.BlockSpec((D_DV, 1), lambda b, p, i: (0, 0))],
        out_specs=pl.BlockSpec((tq, vw), lambda b, p, i: (b * nq + i, p)),
        scratch_shapes=[pltpu.VMEM((4, 2 * D_DH, tq), BF16),
                        pltpu.VMEM((KC, tq), F32),
                        pltpu.VMEM((4, 1, tq), F32), pltpu.VMEM((4, 1, tq), F32),
                        pltpu.VMEM((4, D_DV, tq), F32)],
        compiler_params=_cparams(("parallel", "parallel", "arbitrary")),
        name="diff_attn",
    )(slopes, fm_d, fm_d, main, main, fm_d, lq1, lk1, lq2, lk2, subln_g)


def _merge_kernel(x_ref, gl_ref, ya_ref, yb_ref, yc_ref, yd_ref, wup_ref, wout_ref, o_ref):
    merged = None
    for g, y_ref in enumerate((ya_ref, yb_ref, yc_ref, yd_ref)):
        u = jnp.dot(y_ref[...], wup_ref[g], preferred_element_type=F32)
        gate = gl_ref[:, g * D_MODEL:(g + 1) * D_MODEL].astype(F32)
        term = _sigmoid(gate) * u
        merged = term if merged is None else merged + term
    o_ref[...] = x_ref[...] + jnp.dot(merged.astype(BF16), wout_ref[...], preferred_element_type=F32)


def _merge(x2, main, ya, yb, yc, yd, w_up, w_out):
    t = x2.shape[0]
    tm = min(512, t)
    ytile = pl.BlockSpec((tm, 512), lambda i: (i, 0))
    return pl.pallas_call(
        _merge_kernel,
        out_shape=jax.ShapeDtypeStruct((t, D_MODEL), F32),
        grid=(t // tm,),
        in_specs=[pl.BlockSpec((tm, D_MODEL), lambda i: (i, 0)),
                  pl.BlockSpec((tm, N_BRANCH * D_MODEL), lambda i: (i, 0)),
                  ytile, ytile, ytile, ytile,
                  pl.BlockSpec((N_BRANCH, 512, D_MODEL), lambda i: (0, 0, 0)),
                  pl.BlockSpec((D_MODEL, D_MODEL), lambda i: (0, 0))],
        out_specs=pl.BlockSpec((tm, D_MODEL), lambda i: (i, 0)),
        compiler_params=_cparams(("parallel",)),
        name="merge",
    )(x2, main, ya, yb, yc, yd, w_up, w_out)


def _ffn_kernel(x_ref, g_ref, wg_ref, wu_ref, wd_ref, fg_ref, o_ref, *, final):
    x = x_ref[...]
    ms = jnp.mean(x * x, axis=-1, keepdims=True)
    h = (x * lax.rsqrt(ms + EPS) * g_ref[...]).astype(BF16)
    acc = x
    for a, b in FF_SPLITS:
        gt = jnp.dot(h, wg_ref[:, a:b], preferred_element_type=F32)
        up = jnp.dot(h, wu_ref[:, a:b], preferred_element_type=F32)
        act = (gt * _sigmoid(gt)) * up
        acc = acc + jnp.dot(act.astype(BF16), wd_ref[a:b, :], preferred_element_type=F32)
    if final:
        ms2 = jnp.mean(acc * acc, axis=-1, keepdims=True)
        acc = acc * lax.rsqrt(ms2 + EPS) * fg_ref[...]
    o_ref[...] = acc


def _ffn(x2, g, wg, wu, wd, fg, final):
    t = x2.shape[0]
    tm = min(512, t)
    return pl.pallas_call(
        functools.partial(_ffn_kernel, final=final),
        out_shape=jax.ShapeDtypeStruct((t, D_MODEL), F32),
        grid=(t // tm,),
        in_specs=[pl.BlockSpec((tm, D_MODEL), lambda i: (i, 0)),
                  pl.BlockSpec((1, D_MODEL), lambda i: (0, 0)),
                  pl.BlockSpec((D_MODEL, D_FF), lambda i: (0, 0)),
                  pl.BlockSpec((D_MODEL, D_FF), lambda i: (0, 0)),
                  pl.BlockSpec((D_FF, D_MODEL), lambda i: (0, 0)),
                  pl.BlockSpec((1, D_MODEL), lambda i: (0, 0))],
        out_specs=pl.BlockSpec((tm, D_MODEL), lambda i: (i, 0)),
        compiler_params=_cparams(("parallel",)),
        name="ffn",
    )(x2, g, wg, wu, wd, fg)


def _rope_tables(seq):
    tpos = jnp.arange(seq)
    row = (tpos // GRID_W).astype(F32)
    colp = (tpos % GRID_W).astype(F32)
    n_freq = B_DH // 4
    inv = ROPE_THETA ** (-jnp.arange(n_freq, dtype=F32) / n_freq)
    ar = (row[:, None] * inv).T
    ac = (colp[:, None] * inv).T
    cos = jnp.concatenate([jnp.cos(ar), jnp.cos(ar), jnp.cos(ac), jnp.cos(ac)], axis=0)
    sin = jnp.concatenate([-jnp.sin(ar), jnp.sin(ar), -jnp.sin(ac), jnp.sin(ac)], axis=0)
    return cos.astype(F32), sin.astype(F32)


def _w_cols(w_in, name, scale=None):
    a, b = _OFF[name]
    w = w_in[:, a:b]
    return w if scale is None else w * scale


def _layer(x2, l, bsz, seq, cos, sin, norm1_g, w_in, a_conv_w, a_gate_bias, a_norm_g, b_qnorm_g, b_knorm_g,
           c_rpb, d_lq1, d_lk1, d_lq2, d_lk2, d_subln_g, w_up_a, w_up_b, w_up_c, w_up_d, w_out,
           norm2_g, w_ffn_gate, w_ffn_up, w_ffn_down, final_g, final):
    scales = {"cq": C_DH ** -0.5, "dq": D_DH ** -0.5}
    w_main = jnp.concatenate([_w_cols(w_in, n, scales.get(n)) for n in _MAIN_ORDER], axis=1).astype(BF16)
    ga, _ = _OFF["ag"]
    gate_cols = [ga + ty * A_HEADS + 2 * p + hh for p in range(2) for ty in range(4) for hh in range(2)]
    wg = w_in[:, jnp.asarray(gate_cols)].reshape(D_MODEL, 2, 8)
    w_gate = jnp.pad(wg, ((0, 0), (0, 0), (0, 120))).reshape(D_MODEL, 256).astype(BF16)
    gb = a_gate_bias[jnp.asarray([c - ga for c in gate_cols])].reshape(2, 8)
    gate_bias = jnp.pad(gb, ((0, 0), (0, 120))).reshape(1, 256).astype(F32)
    wt_b = jnp.concatenate([_w_cols(w_in, n) for n in ("bq", "bk", "bv")], axis=1).T.astype(BF16)
    wt_d = jnp.concatenate([_w_cols(w_in, "dq", scales["dq"]), _w_cols(w_in, "dv")], axis=1).T.astype(BF16)
    cw = a_conv_w.astype(F32)
    conv_w = jnp.stack([jnp.concatenate([cw[:, p * 256:(p + 1) * 256],
                                         cw[:, A_WIDTH + p * 256:A_WIDTH + (p + 1) * 256]], axis=1)
                        for p in range(2)], axis=0)

    main, gates, h = _proj(x2, norm1_g.reshape(1, D_MODEL).astype(F32), w_main, w_gate)
    fm_b = _projt(h, wt_b, F32, "proj_fm_b")
    fm_d = _projt(h, wt_d, BF16, "proj_fm_d")

    y_a = _mlstm(main, gates, gate_bias, conv_w, a_norm_g.reshape(1, A_WIDTH).astype(F32), bsz, seq)

    qt_b, k_b, v_b = _gqa_prep(fm_b, cos, sin, b_qnorm_g.reshape(B_DH, 1).astype(F32),
                               b_knorm_g.reshape(B_DH, 1).astype(F32), seq)
    y_b = _gqa_attn(qt_b, k_b, v_b, bsz, seq)

    y_c = _natten(main, _natten_bias_table(c_rpb, seq // GRID_W), bsz, seq)

    lambda_init = 0.8 - 0.6 * math.exp(-0.3 * l)
    vec = lambda a: a.reshape(1, D_DH).astype(F32)
    y_d = _diff_attn(fm_d, main, vec(d_lq1), vec(d_lk1), vec(d_lq2), vec(d_lk2),
                     d_subln_g.reshape(D_DV, 1).astype(F32), lambda_init, bsz, seq)

    w_up = jnp.stack([w_up_a, w_up_b, w_up_c, w_up_d], axis=0).astype(BF16)
    x2 = _merge(x2, main, y_a, y_b, y_c, y_d, w_up, w_out.astype(BF16))
    return _ffn(x2, norm2_g.reshape(1, D_MODEL).astype(F32), w_ffn_gate.astype(BF16), w_ffn_up.astype(BF16),
                w_ffn_down.astype(BF16), final_g.reshape(1, D_MODEL).astype(F32), final)


def kernel(x, norm1_g, w_in, a_conv_w, a_gate_bias, a_norm_g, b_qnorm_g, b_knorm_g, c_rpb, d_lambda_q1, d_lambda_k1, d_lambda_q2, d_lambda_k2, d_subln_g, w_up_a, w_up_b, w_up_c, w_up_d, w_out, norm2_g, w_ffn_gate, w_ffn_up, w_ffn_down, final_g):
    bsz, seq, _ = x.shape
    assert seq % KC == 0 and seq % GRID_W == 0
    cos, sin = _rope_tables(seq)
    x2 = x.reshape(bsz * seq, D_MODEL)
    depth = norm1_g.shape[0]
    for l in range(depth):
        x2 = _layer(x2, l, bsz, seq, cos, sin, norm1_g[l], w_in[l], a_conv_w[l], a_gate_bias[l], a_norm_g[l],
                    b_qnorm_g[l], b_knorm_g[l], c_rpb[l], d_lambda_q1[l], d_lambda_k1[l], d_lambda_q2[l],
                    d_lambda_k2[l], d_subln_g[l], w_up_a[l], w_up_b[l], w_up_c[l], w_up_d[l], w_out[l],
                    norm2_g[l], w_ffn_gate[l], w_ffn_up[l], w_ffn_down[l], final_g, l == depth - 1)
    return x2.reshape(bsz, seq, D_MODEL)
```

```python
import functools
import math

import jax
import jax.numpy as jnp
import numpy as np
from jax import lax
from jax.experimental import pallas as pl
from jax.experimental.pallas import tpu as pltpu

F32 = jnp.float32
BF16 = jnp.bfloat16

D_MODEL = 1024
DEPTH = 2
GRID_W = 64
EPS = 1e-6
N_BRANCH = 4
A_HEADS, A_DH, A_CHUNK = 4, 128, 128
A_WIDTH = A_HEADS * A_DH
B_HEADS, B_KV, B_DH = 8, 2, 64
B_WIDTH = B_HEADS * B_DH
ROPE_THETA = 10000.0
C_HEADS, C_DH, C_WIN_R, C_WIN_C = 8, 64, 8, 16
C_WIDTH = C_HEADS * C_DH
D_HEADS, D_DH = 4, 64
D_DV = 2 * D_DH
D_WIDTH = D_HEADS * D_DV
D_FF = ((8 * D_MODEL + 3 * 256 - 1) // (3 * 256)) * 256
FF_SPLITS = ((0, 1536), (1536, D_FF))

_SIZES = (A_WIDTH, A_WIDTH, A_WIDTH, A_WIDTH, 4 * A_HEADS, B_WIDTH, B_KV * B_DH, B_KV * B_DH,
          C_WIDTH, C_WIDTH, C_WIDTH, 2 * D_HEADS * D_DH, 2 * D_HEADS * D_DH, D_WIDTH, N_BRANCH * D_MODEL)
_NAMES = ("aq", "ak", "av", "ao", "ag", "bq", "bk", "bv", "cq", "ck", "cv", "dq", "dk", "dv", "gl")
_OFF = {}
_o = 0
for _n, _s in zip(_NAMES, _SIZES):
    _OFF[_n] = (_o, _o + _s)
    _o += _s

_MAIN_ORDER = ("gl", "aq", "ak", "av", "ao", "cq", "ck", "cv", "dk")
_MAIN_OFF = {}
_o = 0
for _n in _MAIN_ORDER:
    _MAIN_OFF[_n] = _o
    _o += _OFF[_n][1] - _OFF[_n][0]
MAIN_N = _o

KC = 512
VMEM_LIMIT = 56 * 1024 * 1024


def _cparams(sem, vmem=VMEM_LIMIT):
    return pltpu.CompilerParams(dimension_semantics=sem, vmem_limit_bytes=vmem)


def _sigmoid(x):
    return 1.0 / (1.0 + jnp.exp(-x))


def _aligned(x, m):
    return x if isinstance(x, int) else pl.multiple_of(x, m)


def _proj_kernel(x_ref, g_ref, w_ref, wg_ref, o_ref, og_ref, h_ref):
    @pl.when(pl.program_id(1) == 0)
    def _():
        x = x_ref[...]
        ms = jnp.mean(x * x, axis=-1, keepdims=True)
        hb = (x * lax.rsqrt(ms + EPS) * g_ref[...]).astype(BF16)
        h_ref[...] = hb
        og_ref[...] = jnp.dot(hb, wg_ref[...], preferred_element_type=F32)

    o_ref[...] = jnp.dot(h_ref[...], w_ref[...], preferred_element_type=F32).astype(o_ref.dtype)


def _proj(x2, g, w_main, w_gate):
    t = x2.shape[0]
    tm = min(1024, t)
    tn = 1024
    n = w_main.shape[1]
    ng = w_gate.shape[1]
    return pl.pallas_call(
        _proj_kernel,
        out_shape=(jax.ShapeDtypeStruct((t, n), BF16),
                   jax.ShapeDtypeStruct((t, ng), F32),
                   jax.ShapeDtypeStruct((t, D_MODEL), BF16)),
        grid=(t // tm, n // tn),
        in_specs=[pl.BlockSpec((tm, D_MODEL), lambda i, j: (i, 0)),
                  pl.BlockSpec((1, D_MODEL), lambda i, j: (0, 0)),
                  pl.BlockSpec((D_MODEL, tn), lambda i, j: (0, j)),
                  pl.BlockSpec((D_MODEL, ng), lambda i, j: (0, 0))],
        out_specs=(pl.BlockSpec((tm, tn), lambda i, j: (i, j)),
                   pl.BlockSpec((tm, ng), lambda i, j: (i, 0)),
                   pl.BlockSpec((tm, D_MODEL), lambda i, j: (i, 0))),
        compiler_params=_cparams(("parallel", "arbitrary")),
        name="proj_main",
    )(x2, g, w_main, w_gate)


def _projt_kernel(h_ref, wt_ref, o_ref, *, nchunk):
    for c in range(nchunk):
        hc = h_ref[c * KC:(c + 1) * KC, :]
        o_ref[c] = lax.dot_general(wt_ref[...], hc, (((1,), (1,)), ((), ())),
                                   preferred_element_type=F32).astype(o_ref.dtype)


def _projt(h, wt, out_dtype, name):
    t = h.shape[0]
    n = wt.shape[0]
    tm = min(2048, t)
    tn = min(768, n) if n % 768 == 0 else 512
    nchunk = tm // KC
    return pl.pallas_call(
        functools.partial(_projt_kernel, nchunk=nchunk),
        out_shape=jax.ShapeDtypeStruct((t // KC, n, KC), out_dtype),
        grid=(t // tm, n // tn),
        in_specs=[pl.BlockSpec((tm, D_MODEL), lambda i, j: (i, 0)),
                  pl.BlockSpec((tn, D_MODEL), lambda i, j: (j, 0))],
        out_specs=pl.BlockSpec((nchunk, tn, KC), lambda i, j: (i, j, 0)),
        compiler_params=_cparams(("parallel", "arbitrary")),
        name=name,
    )(h, wt)


def _log_sigmoid(x):
    return jnp.minimum(x, 0.0) - jnp.log1p(jnp.exp(-jnp.abs(x)))


def _tri_dot(mat, x):
    hi = x.astype(BF16)
    r1 = x - hi.astype(F32)
    mid = r1.astype(BF16)
    lo = (r1 - mid.astype(F32)).astype(BF16)
    return (jnp.dot(mat, hi, preferred_element_type=F32)
            + jnp.dot(mat, mid, preferred_element_type=F32)
            + jnp.dot(mat, lo, preferred_element_type=F32))


def _mlstm_kernel(q_ref, k_ref, v_ref, o_ref, g_ref, gb_ref, cw_ref, ng_ref, y_ref,
                  qs_ref, ks_ref, hs_ref, c_ref, n_ref, m_ref, *, seq):
    L = A_CHUNK
    nc = seq // L
    hw = 2 * A_DH

    rowi = lax.broadcasted_iota(jnp.int32, (L, hw), 0)

    def conv_body(c, carry):
        r0 = pl.multiple_of(c * L, L)
        pstart = pl.multiple_of(jnp.maximum(r0 - 16, 0), 16)
        nstart = pl.multiple_of(jnp.minimum(r0 + L, seq - 16), 16)
        has_prev = jnp.where(c > 0, 1.0, 0.0)
        has_next = jnp.where(c < nc - 1, 1.0, 0.0)
        for src, dst, woff, scale in ((q_ref, qs_ref, 0, 1.0), (k_ref, ks_ref, hw, A_DH ** -0.5)):
            xc = src[pl.ds(r0, L), :].astype(F32)
            prev = src[pl.ds(pstart, 16), :].astype(F32)[15:16] * has_prev
            nxt = src[pl.ds(nstart, 16), :].astype(F32)[0:1] * has_next
            xp = jnp.where(rowi == 0, prev, pltpu.roll(xc, 1, 0))
            xn = jnp.where(rowi == L - 1, nxt, pltpu.roll(xc, L - 1, 0))
            w = cw_ref[:, woff:woff + hw]
            y = xp * w[0:1] + xc * w[1:2] + xn * w[2:3]
            y = y * _sigmoid(y)
            dst[pl.ds(r0, L), :] = (y * scale).astype(BF16)
        return carry

    lax.fori_loop(0, nc, conv_body, 0)

    c_ref[...] = jnp.zeros(c_ref.shape, F32)
    n_ref[...] = jnp.zeros(n_ref.shape, F32)
    m_ref[...] = jnp.zeros(m_ref.shape, F32)

    ti = lax.broadcasted_iota(jnp.int32, (L, L), 0)
    tj = lax.broadcasted_iota(jnp.int32, (L, L), 1)
    lower = tj <= ti
    upper = tj >= ti
    lmat = jnp.where(lower, 1.0, 0.0).astype(BF16)
    umat = jnp.where(upper, 1.0, 0.0).astype(BF16)

    def direction(c, d):
        r0 = pl.multiple_of(c * L, L)
        gcol = g_ref[pl.ds(r0, L), :] + gb_ref[...]
        bcol_all = _tri_dot(lmat if d == 0 else umat, _log_sigmoid(gcol))
        grow = gcol.T
        brow_all = bcol_all.T
        mask = lower if d == 0 else upper
        outs = []
        for hh in range(2):
            il = 4 * d + hh
            fl = il + 2
            idx = 2 * d + hh
            icol = gcol[:, il:il + 1]
            bcol = bcol_all[:, fl:fl + 1]
            irow = grow[il:il + 1, :]
            brow = brow_all[fl:fl + 1, :]
            m_prev = m_ref[idx][:, 0:1]
            dmat = jnp.where(mask, bcol - brow + irow, -jnp.inf)
            inter = bcol + m_prev
            mt = jnp.maximum(inter, jnp.max(dmat, axis=1, keepdims=True))
            w_inter = jnp.exp(inter - mt)
            q = qs_ref[pl.ds(r0, L), hh * A_DH:(hh + 1) * A_DH]
            k = ks_ref[pl.ds(r0, L), hh * A_DH:(hh + 1) * A_DH]
            v = v_ref[pl.ds(r0, L), hh * A_DH:(hh + 1) * A_DH]
            qk = lax.dot_general(q, k, (((1,), (1,)), ((), ())), preferred_element_type=F32)
            sqk = qk * jnp.exp(dmat - mt)
            cmat = c_ref[idx]
            nrow = n_ref[idx]
            num = (w_inter * jnp.dot(q, cmat.astype(BF16), preferred_element_type=F32)
                   + jnp.dot(sqk.astype(BF16), v, preferred_element_type=F32))
            den = (w_inter * jnp.sum(q.astype(F32) * nrow, axis=1, keepdims=True)
                   + jnp.sum(sqk, axis=1, keepdims=True))
            outs.append(num / jnp.maximum(jnp.abs(den), jnp.exp(-mt)))
            bl = bcol[L - 1:L] if d == 0 else bcol[0:1]
            gvec = bl - bcol + icol
            m_new = jnp.maximum(bl + m_prev, jnp.max(gvec, axis=0, keepdims=True))
            wc = jnp.exp(bl + m_prev - m_new)
            kw = k.astype(F32) * jnp.exp(gvec - m_new)
            c_ref[idx] = wc * cmat + jnp.dot(kw.T.astype(BF16), v, preferred_element_type=F32)
            n_ref[idx] = wc * nrow + jnp.sum(kw, axis=0, keepdims=True)
            m_ref[idx] = jnp.broadcast_to(m_new, (1, 128))
        return jnp.concatenate(outs, axis=1)

    def finalize(tot, r0):
        parts = []
        for hh in range(2):
            hcur = tot[:, hh * A_DH:(hh + 1) * A_DH]
            mu = jnp.mean(hcur, axis=1, keepdims=True)
            cen = hcur - mu
            var = jnp.mean(cen * cen, axis=1, keepdims=True)
            parts.append(cen * lax.rsqrt(var + EPS))
        hn = jnp.concatenate(parts, axis=1) * ng_ref[...]
        y_ref[pl.ds(r0, L), :] = (hn * _sigmoid(o_ref[pl.ds(r0, L), :].astype(F32))).astype(y_ref.dtype)

    def first_half(j, carry):
        for d, c in ((0, j), (1, nc - 1 - j)):
            hs_ref[pl.ds(pl.multiple_of(c * L, L), L), :] = direction(c, d)
        return carry

    def second_half(j, carry):
        for d, c in ((0, j), (1, nc - 1 - j)):
            r0 = pl.multiple_of(c * L, L)
            finalize(hs_ref[pl.ds(r0, L), :] + direction(c, d), r0)
        return carry

    lax.fori_loop(0, nc // 2, first_half, 0)
    lax.fori_loop(nc // 2, nc, second_half, 0)


def _mlstm(main, gates, gate_bias, conv_w, norm_g, bsz, seq):
    t = bsz * seq
    hw = 2 * A_DH
    assert (seq // A_CHUNK) % 2 == 0

    def col(name):
        base = _MAIN_OFF[name] // hw
        return pl.BlockSpec((seq, hw), lambda b, p: (b, base + p))

    return pl.pallas_call(
        functools.partial(_mlstm_kernel, seq=seq),
        out_shape=jax.ShapeDtypeStruct((t, A_WIDTH), BF16),
        grid=(bsz, 2),
        in_specs=[col("aq"), col("ak"), col("av"), col("ao"),
                  pl.BlockSpec((seq, 128), lambda b, p: (b, p)),
                  pl.BlockSpec((1, 128), lambda b, p: (0, p)),
                  pl.BlockSpec((None, 3, 2 * hw), lambda b, p: (p, 0, 0)),
                  pl.BlockSpec((1, hw), lambda b, p: (0, p))],
        out_specs=pl.BlockSpec((seq, hw), lambda b, p: (b, p)),
        scratch_shapes=[pltpu.VMEM((seq, hw), BF16), pltpu.VMEM((seq, hw), BF16),
                        pltpu.VMEM((seq, hw), F32),
                        pltpu.VMEM((4, A_DH, A_DH), F32), pltpu.VMEM((4, 1, A_DH), F32),
                        pltpu.VMEM((4, 1, 128), F32)],
        compiler_params=_cparams(("parallel", "arbitrary")),
        name="mlstm",
    )(main, main, main, main, gates, gate_bias, conv_w, norm_g)


LOG2E = 1.4426950408889634


def _stash_scores(s, s_ref, mx_ref, idx):
    tk, tq = s.shape
    s_ref[idx] = s
    mx_ref[idx] = jnp.max(s.reshape(tk // 8, 8, tq), axis=0)


def _flash_update(s_ref, mx_ref, vc, m_ref, l_ref, acc_ref, idx):
    tk, tq = s_ref.shape[1:]
    m_old = m_ref[idx]
    m_new = jnp.maximum(m_old, jnp.max(mx_ref[idx], axis=0, keepdims=True))
    alpha = jnp.exp2(m_old - m_new)
    p = jnp.exp2(s_ref[idx] - m_new)
    l_ref[idx] = alpha * l_ref[idx] + jnp.sum(p.reshape(tk // 8, 8, tq), axis=0)
    acc_ref[idx] = alpha * acc_ref[idx] + jnp.dot(vc, p.astype(BF16), preferred_element_type=F32)
    m_ref[idx] = m_new


def _pipelined_chunks(nkc, scores, consume):
    assert nkc % 2 == 0
    scores(0, 0)

    def body(j, carry):
        c = 2 * j
        scores(c + 1, 1)
        consume(c, 0)
        scores(c + 2, 0)
        consume(c + 1, 1)
        return carry

    lax.fori_loop(0, nkc // 2 - 1, body, 0)
    scores(nkc - 1, 1)
    consume(nkc - 2, 0)
    consume(nkc - 1, 1)


def _gqa_prep_kernel(fm_ref, cos_ref, sin_ref, qg_ref, kg_ref, q_ref, k_ref, v_ref):
    x = fm_ref[0]
    cos = cos_ref[...]
    sin = sin_ref[...]

    def norm_rope(xh, g):
        ms = jnp.mean(xh * xh, axis=0, keepdims=True)
        xn = xh * lax.rsqrt(ms + EPS) * g
        partner = jnp.concatenate([xn[16:32], xn[0:16], xn[48:64], xn[32:48]], axis=0)
        return xn * cos + partner * sin

    for h in range(B_HEADS):
        qh = norm_rope(x[h * B_DH:(h + 1) * B_DH], qg_ref[...]) * (B_DH ** -0.5 * LOG2E)
        q_ref[0, h * B_DH:(h + 1) * B_DH, :] = qh.astype(BF16)
    ks = [norm_rope(x[B_WIDTH + g * B_DH:B_WIDTH + (g + 1) * B_DH], kg_ref[...]) for g in range(B_KV)]
    k_ref[...] = jnp.concatenate(ks, axis=0).T.astype(BF16)
    v_ref[0] = x[B_WIDTH + B_KV * B_DH:].astype(BF16)


def _gqa_prep(fm_b, cos, sin, qg, kg, seq):
    nchunks = fm_b.shape[0]
    t = nchunks * KC
    spc = seq // KC
    kvw = B_KV * B_DH
    return pl.pallas_call(
        _gqa_prep_kernel,
        out_shape=(jax.ShapeDtypeStruct((nchunks, B_WIDTH, KC), BF16),
                   jax.ShapeDtypeStruct((t, kvw), BF16),
                   jax.ShapeDtypeStruct((nchunks, kvw, KC), BF16)),
        grid=(nchunks,),
        in_specs=[pl.BlockSpec((1, B_WIDTH + 2 * kvw, KC), lambda i: (i, 0, 0)),
                  pl.BlockSpec((B_DH, KC), lambda i: (0, i % spc)),
                  pl.BlockSpec((B_DH, KC), lambda i: (0, i % spc)),
                  pl.BlockSpec((B_DH, 1), lambda i: (0, 0)),
                  pl.BlockSpec((B_DH, 1), lambda i: (0, 0))],
        out_specs=(pl.BlockSpec((1, B_WIDTH, KC), lambda i: (i, 0, 0)),
                   pl.BlockSpec((KC, kvw), lambda i: (i, 0)),
                   pl.BlockSpec((1, kvw, KC), lambda i: (i, 0, 0))),
        compiler_params=_cparams(("parallel",)),
        name="gqa_prep",
    )(fm_b, cos, sin, qg, kg)


def _gqa_attn_kernel(q_ref, k_ref, v_ref, o_ref, qpad_ref, s0_ref, s1_ref, mx0_ref, mx1_ref,
                     m_ref, l_ref, acc_ref, *, nkc, tq):
    g = pl.program_id(1)
    rep = B_HEADS // B_KV
    row = lax.broadcasted_iota(jnp.int32, (2 * B_DH, tq), 0)
    sel = (row // B_DH) == g
    for r in range(rep):
        qh = q_ref[0, r * B_DH:(r + 1) * B_DH, :].astype(F32)
        qpad_ref[r] = jnp.where(sel, jnp.concatenate([qh, qh], axis=0), 0.0).astype(BF16)
    m_ref[...] = jnp.full(m_ref.shape, -jnp.inf, F32)
    l_ref[...] = jnp.zeros(l_ref.shape, F32)
    acc_ref[...] = jnp.zeros(acc_ref.shape, F32)
    slots = ((s0_ref, mx0_ref), (s1_ref, mx1_ref))

    def scores(c, slot):
        s_ref, mx_ref = slots[slot]
        kc = k_ref[pl.ds(_aligned(c * KC, KC), KC), :]
        for r in range(rep):
            _stash_scores(jnp.dot(kc, qpad_ref[r], preferred_element_type=F32), s_ref, mx_ref, r)

    def consume(c, slot):
        s_ref, mx_ref = slots[slot]
        vc = v_ref[c]
        for r in range(rep):
            _flash_update(s_ref, mx_ref, vc, m_ref, l_ref, acc_ref, r)

    _pipelined_chunks(nkc, scores, consume)
    outs = [acc_ref[r] / jnp.sum(l_ref[r], axis=0, keepdims=True) for r in range(rep)]
    o_ref[...] = jnp.concatenate(outs, axis=0).T.astype(o_ref.dtype)


def _gqa_attn(qt, k_tok, v_fm, bsz, seq):
    t = bsz * seq
    tq = min(256, seq)
    nq = seq // tq
    nkc = seq // KC
    qpc = KC // tq
    rep = B_HEADS // B_KV
    gw = rep * B_DH
    return pl.pallas_call(
        functools.partial(_gqa_attn_kernel, nkc=nkc, tq=tq),
        out_shape=jax.ShapeDtypeStruct((t, B_WIDTH), BF16),
        grid=(bsz, B_KV, nq),
        in_specs=[pl.BlockSpec((1, gw, tq), lambda b, g, i: (b * nkc + i // qpc, g, i % qpc)),
                  pl.BlockSpec((seq, B_KV * B_DH), lambda b, g, i: (b, 0)),
                  pl.BlockSpec((nkc, B_DH, KC), lambda b, g, i: (b, g, 0))],
        out_specs=pl.BlockSpec((tq, gw), lambda b, g, i: (b * nq + i, g)),
        scratch_shapes=[pltpu.VMEM((rep, 2 * B_DH, tq), BF16),
                        pltpu.VMEM((rep, KC, tq), F32), pltpu.VMEM((rep, KC, tq), F32),
                        pltpu.VMEM((rep, 8, tq), F32), pltpu.VMEM((rep, 8, tq), F32),
                        pltpu.VMEM((rep, 1, tq), F32), pltpu.VMEM((rep, 8, tq), F32),
                        pltpu.VMEM((rep, B_DH, tq), F32)],
        compiler_params=_cparams(("parallel", "parallel", "arbitrary")),
        name="gqa_attn",
    )(qt, k_tok, v_fm)


def _natten_kernel(q_ref, k_ref, v_ref, bias_ref, o_ref, *, rows):
    r = pl.program_id(1)
    rs = jnp.clip(r - C_WIN_R // 2, 0, rows - C_WIN_R)
    start = pl.multiple_of(rs * GRID_W, GRID_W)
    nkeys = C_WIN_R * GRID_W
    npair = C_HEADS // 2
    own = (lax.broadcasted_iota(jnp.int32, (2 * GRID_W, 2 * C_DH), 0) // GRID_W
           == lax.broadcasted_iota(jnp.int32, (2 * GRID_W, 2 * C_DH), 1) // C_DH)
    scores = []
    for p in range(npair):
        kp = k_ref[pl.ds(start, nkeys), p * 128:(p + 1) * 128]
        qp = q_ref[:, p * 128:(p + 1) * 128].astype(F32)
        qm = jnp.where(own, jnp.concatenate([qp, qp], axis=0), 0.0).astype(BF16)
        s = lax.dot_general(qm, kp, (((1,), (1,)), ((), ())), preferred_element_type=F32)
        scores.append(s + bias_ref[0, 2 * p:2 * p + 2].reshape(2 * GRID_W, nkeys))
    probs = []
    for s in scores:
        e = jnp.exp(s - jnp.max(s, axis=1, keepdims=True))
        probs.append((e.astype(BF16), jnp.sum(e, axis=1, keepdims=True)))
    outs = []
    for p, (e, l) in enumerate(probs):
        vp = v_ref[pl.ds(start, nkeys), p * 128:(p + 1) * 128]
        o = jnp.where(own, jnp.dot(e, vp, preferred_element_type=F32) / l, 0.0)
        outs.append(o[:GRID_W] + o[GRID_W:])
    o_ref[...] = jnp.concatenate(outs, axis=1).astype(o_ref.dtype)


def _natten_bias_table(rpb, rows):
    c = np.arange(GRID_W)[:, None]
    kc = np.arange(GRID_W)[None, :]
    cs = np.clip(c - C_WIN_C // 2, 0, GRID_W - C_WIN_C)
    valid = (kc >= cs) & (kc < cs + C_WIN_C)
    nd = 2 * C_WIN_C - 1
    onehot = (kc - c + (C_WIN_C - 1))[None] == np.arange(nd)[:, None, None]
    toe = jnp.sum(jnp.where(onehot[None, None], rpb.astype(F32)[:, :, :, None, None], 0.0), axis=2)
    toe = jnp.where(valid[None, None], toe, -1e30)
    tbl = jnp.stack([toe[:, C_WIN_R - 1 - v:2 * C_WIN_R - 1 - v] for v in range(C_WIN_R)], axis=0)
    return jnp.swapaxes(tbl, 2, 3).reshape(C_WIN_R, C_HEADS, GRID_W, C_WIN_R * GRID_W)


def _natten(main, bias_tbl, bsz, seq):
    t = bsz * seq
    rows = seq // GRID_W
    assert rows >= C_WIN_R
    half = C_WIN_R // 2

    def col(name):
        return _MAIN_OFF[name] // C_WIDTH

    cq, ck, cv = col("cq"), col("ck"), col("cv")
    return pl.pallas_call(
        functools.partial(_natten_kernel, rows=rows),
        out_shape=jax.ShapeDtypeStruct((t, C_WIDTH), BF16),
        grid=(bsz, rows),
        in_specs=[pl.BlockSpec((GRID_W, C_WIDTH), lambda b, r: (b * rows + r, cq)),
                  pl.BlockSpec((seq, C_WIDTH), lambda b, r: (b, ck)),
                  pl.BlockSpec((seq, C_WIDTH), lambda b, r: (b, cv)),
                  pl.BlockSpec((1, C_HEADS, GRID_W, C_WIN_R * GRID_W),
                               lambda b, r: (r - jnp.clip(r - half, 0, rows - C_WIN_R), 0, 0, 0))],
        out_specs=pl.BlockSpec((GRID_W, C_WIDTH), lambda b, r: (b * rows + r, 0)),
        compiler_params=_cparams(("parallel", "arbitrary")),
        name="natten",
    )(main, main, main, bias_tbl)


def _diff_attn_kernel(slopes_ref, q1_ref, q2_ref, k1_ref, k2_ref, v_ref, lq1_ref, lk1_ref, lq2_ref, lk2_ref,
                      sg_ref, o_ref, qpad_ref, e_ref, s0_ref, s1_ref, mx0_ref, mx1_ref, m_ref, l_ref, acc_ref,
                      *, nkc, tq, lambda_init):
    pair = pl.program_id(1)
    qi = pl.program_id(2)
    row = lax.broadcasted_iota(jnp.int32, (2 * D_DH, tq), 0)
    for c, qr in enumerate((q1_ref, q2_ref)):
        qf = qr[0].astype(F32)
        for hh in range(2):
            qpad_ref[2 * c + hh] = jnp.where((row // D_DH) == hh, qf, 0.0).astype(BF16)
    m_ref[...] = jnp.full(m_ref.shape, -jnp.inf, F32)
    l_ref[...] = jnp.zeros(l_ref.shape, F32)
    acc_ref[...] = jnp.zeros(acc_ref.shape, F32)
    kk = lax.broadcasted_iota(jnp.int32, (KC, tq), 0)
    qq = lax.broadcasted_iota(jnp.int32, (KC, tq), 1)
    e_ref[...] = (kk - qq).astype(F32)
    slots = ((s0_ref, mx0_ref), (s1_ref, mx1_ref))
    krefs = (k1_ref, k2_ref)

    def scores(ci, slot):
        s_ref, mx_ref = slots[slot]
        off = (ci * KC - qi * tq).astype(F32)
        dist = jnp.abs(e_ref[...] + off)
        r0 = _aligned(ci * KC, KC)
        for hh in range(2):
            bias = dist * (-slopes_ref[2 * pair + hh])
            for c in range(2):
                s = jnp.dot(krefs[c][pl.ds(r0, KC), :], qpad_ref[2 * c + hh], preferred_element_type=F32) + bias
                _stash_scores(s, s_ref, mx_ref, 2 * c + hh)

    def consume(ci, slot):
        s_ref, mx_ref = slots[slot]
        for hh in range(2):
            vc = v_ref[ci, hh * D_DV:(hh + 1) * D_DV, :]
            for c in range(2):
                _flash_update(s_ref, mx_ref, vc, m_ref, l_ref, acc_ref, 2 * c + hh)

    _pipelined_chunks(nkc, scores, consume)
    lam = (jnp.exp(jnp.sum(lq1_ref[...] * lk1_ref[...], axis=1, keepdims=True))
           - jnp.exp(jnp.sum(lq2_ref[...] * lk2_ref[...], axis=1, keepdims=True)) + lambda_init)
    outs = []
    for hh in range(2):
        o1 = acc_ref[hh] / jnp.sum(l_ref[hh], axis=0, keepdims=True)
        o2 = acc_ref[2 + hh] / jnp.sum(l_ref[2 + hh], axis=0, keepdims=True)
        o = o1 - lam * o2
        ms = jnp.mean(o * o, axis=0, keepdims=True)
        outs.append(o * lax.rsqrt(ms + EPS) * sg_ref[...] * (1.0 - lambda_init))
    o_ref[...] = jnp.concatenate(outs, axis=0).T.astype(o_ref.dtype)


def _diff_attn(fm_d, main, lq1, lk1, lq2, lk2, subln_g, lambda_init, bsz, seq):
    t = bsz * seq
    tq = min(256, seq)
    nq = seq // tq
    nkc = seq // KC
    qpc = KC // tq
    pw = 2 * D_DH
    vw = 2 * D_DV
    dk0 = _MAIN_OFF["dk"] // pw
    vrow0 = (2 * D_HEADS * D_DH) // vw
    slopes = jnp.asarray([LOG2E * 2.0 ** (-8.0 * (h + 1) / D_HEADS) for h in range(D_HEADS)], F32)

    def qspec(c):
        return pl.BlockSpec((1, pw, tq), lambda b, p, i: (b * nkc + i // qpc, 2 * c + p, i % qpc))

    def kspec(c):
        return pl.BlockSpec((seq, pw), lambda b, p, i: (b, dk0 + 2 * c + p))

    vec = pl.BlockSpec((1, D_DH), lambda b, p, i: (0, 0))
    return pl.pallas_call(
        functools.partial(_diff_attn_kernel, nkc=nkc, tq=tq, lambda_init=lambda_init),
        out_shape=jax.ShapeDtypeStruct((t, D_WIDTH), BF16),
        grid=(bsz, D_HEADS // 2, nq),
        in_specs=[pl.BlockSpec(memory_space=pltpu.SMEM),
                  qspec(0), qspec(1), kspec(0), kspec(1),
                  pl.BlockSpec((nkc, vw, KC), lambda b, p, i: (b, vrow0 + p, 0)),
                  vec, vec, vec, vec,
                  pl.BlockSpec((D_DV, 1), lambda b, p, i: (0, 0))],
        out_specs=pl.BlockSpec((tq, vw), lambda b, p, i: (b * nq + i, p)),
        scratch_shapes=[pltpu.VMEM((4, 2 * D_DH, tq), BF16),
                        pltpu.VMEM((KC, tq), F32),
                        pltpu.VMEM((4, KC, tq), F32), pltpu.VMEM((4, KC, tq), F32),
                        pltpu.VMEM((4, 8, tq), F32), pltpu.VMEM((4, 8, tq), F32),
                        pltpu.VMEM((4, 1, tq), F32), pltpu.VMEM((4, 8, tq), F32),
                        pltpu.VMEM((4, D_DV, tq), F32)],
        compiler_params=_cparams(("parallel", "parallel", "arbitrary")),
        name="diff_attn",
    )(slopes, fm_d, fm_d, main, main, fm_d, lq1, lk1, lq2, lk2, subln_g)


def _merge_kernel(x_ref, gl_ref, ya_ref, yb_ref, yc_ref, yd_ref, wup_ref, wout_ref, o_ref):
    merged = None
    for g, y_ref in enumerate((ya_ref, yb_ref, yc_ref, yd_ref)):
        u = jnp.dot(y_ref[...], wup_ref[g], preferred_element_type=F32)
        gate = gl_ref[:, g * D_MODEL:(g + 1) * D_MODEL].astype(F32)
        term = _sigmoid(gate) * u
        merged = term if merged is None else merged + term
    o_ref[...] = x_ref[...] + jnp.dot(merged.astype(BF16), wout_ref[...], preferred_element_type=F32)


def _merge(x2, main, ya, yb, yc, yd, w_up, w_out):
    t = x2.shape[0]
    tm = min(512, t)
    ytile = pl.BlockSpec((tm, 512), lambda i: (i, 0))
    return pl.pallas_call(
        _merge_kernel,
        out_shape=jax.ShapeDtypeStruct((t, D_MODEL), F32),
        grid=(t // tm,),
        in_specs=[pl.BlockSpec((tm, D_MODEL), lambda i: (i, 0)),
                  pl.BlockSpec((tm, N_BRANCH * D_MODEL), lambda i: (i, 0)),
                  ytile, ytile, ytile, ytile,
                  pl.BlockSpec((N_BRANCH, 512, D_MODEL), lambda i: (0, 0, 0)),
                  pl.BlockSpec((D_MODEL, D_MODEL), lambda i: (0, 0))],
        out_specs=pl.BlockSpec((tm, D_MODEL), lambda i: (i, 0)),
        compiler_params=_cparams(("parallel",)),
        name="merge",
    )(x2, main, ya, yb, yc, yd, w_up, w_out)


def _ffn_kernel(x_ref, g_ref, wg_ref, wu_ref, wd_ref, fg_ref, o_ref, *, final):
    x = x_ref[...]
    ms = jnp.mean(x * x, axis=-1, keepdims=True)
    h = (x * lax.rsqrt(ms + EPS) * g_ref[...]).astype(BF16)
    acc = x
    for a, b in FF_SPLITS:
        gt = jnp.dot(h, wg_ref[:, a:b], preferred_element_type=F32)
        up = jnp.dot(h, wu_ref[:, a:b], preferred_element_type=F32)
        act = (gt * _sigmoid(gt)) * up
        acc = acc + jnp.dot(act.astype(BF16), wd_ref[a:b, :], preferred_element_type=F32)
    if final:
        ms2 = jnp.mean(acc * acc, axis=-1, keepdims=True)
        acc = acc * lax.rsqrt(ms2 + EPS) * fg_ref[...]
    o_ref[...] = acc


def _ffn(x2, g, wg, wu, wd, fg, final):
    t = x2.shape[0]
    tm = min(512, t)
    return pl.pallas_call(
        functools.partial(_ffn_kernel, final=final),
        out_shape=jax.ShapeDtypeStruct((t, D_MODEL), F32),
        grid=(t // tm,),
        in_specs=[pl.BlockSpec((tm, D_MODEL), lambda i: (i, 0)),
                  pl.BlockSpec((1, D_MODEL), lambda i: (0, 0)),
                  pl.BlockSpec((D_MODEL, D_FF), lambda i: (0, 0)),
                  pl.BlockSpec((D_MODEL, D_FF), lambda i: (0, 0)),
                  pl.BlockSpec((D_FF, D_MODEL), lambda i: (0, 0)),
                  pl.BlockSpec((1, D_MODEL), lambda i: (0, 0))],
        out_specs=pl.BlockSpec((tm, D_MODEL), lambda i: (i, 0)),
        compiler_params=_cparams(("parallel",)),
        name="ffn",
    )(x2, g, wg, wu, wd, fg)


def _rope_tables(seq):
    tpos = jnp.arange(seq)
    row = (tpos // GRID_W).astype(F32)
    colp = (tpos % GRID_W).astype(F32)
    n_freq = B_DH // 4
    inv = ROPE_THETA ** (-jnp.arange(n_freq, dtype=F32) / n_freq)
    ar = (row[:, None] * inv).T
    ac = (colp[:, None] * inv).T
    cos = jnp.concatenate([jnp.cos(ar), jnp.cos(ar), jnp.cos(ac), jnp.cos(ac)], axis=0)
    sin = jnp.concatenate([-jnp.sin(ar), jnp.sin(ar), -jnp.sin(ac), jnp.sin(ac)], axis=0)
    return cos.astype(F32), sin.astype(F32)


def _w_cols(w_in, name, scale=None):
    a, b = _OFF[name]
    w = w_in[:, a:b]
    return w if scale is None else w * scale


def _layer(x2, l, bsz, seq, cos, sin, norm1_g, w_in, a_conv_w, a_gate_bias, a_norm_g, b_qnorm_g, b_knorm_g,
           c_rpb, d_lq1, d_lk1, d_lq2, d_lk2, d_subln_g, w_up_a, w_up_b, w_up_c, w_up_d, w_out,
           norm2_g, w_ffn_gate, w_ffn_up, w_ffn_down, final_g, final):
    scales = {"cq": C_DH ** -0.5, "dq": D_DH ** -0.5 * LOG2E}
    w_main = jnp.concatenate([_w_cols(w_in, n, scales.get(n)) for n in _MAIN_ORDER], axis=1).astype(BF16)
    ga, _ = _OFF["ag"]
    gate_cols = [ga + ty * A_HEADS + 2 * p + hh for p in range(2) for ty in range(4) for hh in range(2)]
    wg = w_in[:, jnp.asarray(gate_cols)].reshape(D_MODEL, 2, 8)
    w_gate = jnp.pad(wg, ((0, 0), (0, 0), (0, 120))).reshape(D_MODEL, 256).astype(BF16)
    gb = a_gate_bias[jnp.asarray([c - ga for c in gate_cols])].reshape(2, 8)
    gate_bias = jnp.pad(gb, ((0, 0), (0, 120))).reshape(1, 256).astype(F32)
    wt_b = jnp.concatenate([_w_cols(w_in, n) for n in ("bq", "bk", "bv")], axis=1).T.astype(BF16)
    wt_d = jnp.concatenate([_w_cols(w_in, "dq", scales["dq"]), _w_cols(w_in, "dv")], axis=1).T.astype(BF16)
    cw = a_conv_w.astype(F32)
    conv_w = jnp.stack([jnp.concatenate([cw[:, p * 256:(p + 1) * 256],
                                         cw[:, A_WIDTH + p * 256:A_WIDTH + (p + 1) * 256]], axis=1)
                        for p in range(2)], axis=0)

    main, gates, h = _proj(x2, norm1_g.reshape(1, D_MODEL).astype(F32), w_main, w_gate)
    fm_b = _projt(h, wt_b, F32, "proj_fm_b")
    fm_d = _projt(h, wt_d, BF16, "proj_fm_d")

    y_a = _mlstm(main, gates, gate_bias, conv_w, a_norm_g.reshape(1, A_WIDTH).astype(F32), bsz, seq)

    qt_b, k_b, v_b = _gqa_prep(fm_b, cos, sin, b_qnorm_g.reshape(B_DH, 1).astype(F32),
                               b_knorm_g.reshape(B_DH, 1).astype(F32), seq)
    y_b = _gqa_attn(qt_b, k_b, v_b, bsz, seq)

    y_c = _natten(main, _natten_bias_table(c_rpb, seq // GRID_W), bsz, seq)

    lambda_init = 0.8 - 0.6 * math.exp(-0.3 * l)
    vec = lambda a: a.reshape(1, D_DH).astype(F32)
    y_d = _diff_attn(fm_d, main, vec(d_lq1), vec(d_lk1), vec(d_lq2), vec(d_lk2),
                     d_subln_g.reshape(D_DV, 1).astype(F32), lambda_init, bsz, seq)

    w_up = jnp.stack([w_up_a, w_up_b, w_up_c, w_up_d], axis=0).astype(BF16)
    x2 = _merge(x2, main, y_a, y_b, y_c, y_d, w_up, w_out.astype(BF16))
    return _ffn(x2, norm2_g.reshape(1, D_MODEL).astype(F32), w_ffn_gate.astype(BF16), w_ffn_up.astype(BF16),
                w_ffn_down.astype(BF16), final_g.reshape(1, D_MODEL).astype(F32), final)


def kernel(x, norm1_g, w_in, a_conv_w, a_gate_bias, a_norm_g, b_qnorm_g, b_knorm_g, c_rpb, d_lambda_q1, d_lambda_k1, d_lambda_q2, d_lambda_k2, d_subln_g, w_up_a, w_up_b, w_up_c, w_up_d, w_out, norm2_g, w_ffn_gate, w_ffn_up, w_ffn_down, final_g):
    bsz, seq, _ = x.shape
    assert seq % KC == 0 and seq % GRID_W == 0
    cos, sin = _rope_tables(seq)
    x2 = x.reshape(bsz * seq, D_MODEL)
    depth = norm1_g.shape[0]
    for l in range(depth):
        x2 = _layer(x2, l, bsz, seq, cos, sin, norm1_g[l], w_in[l], a_conv_w[l], a_gate_bias[l], a_norm_g[l],
                    b_qnorm_g[l], b_knorm_g[l], c_rpb[l], d_lambda_q1[l], d_lambda_k1[l], d_lambda_q2[l],
                    d_lambda_k2[l], d_subln_g[l], w_up_a[l], w_up_b[l], w_up_c[l], w_up_d[l], w_out[l],
                    norm2_g[l], w_ffn_gate[l], w_ffn_up[l], w_ffn_down[l], final_g, l == depth - 1)
    return x2.reshape(bsz, seq, D_MODEL)
```

```python
import functools
import math

import jax
import jax.numpy as jnp
import numpy as np
from jax import lax
from jax.experimental import pallas as pl
from jax.experimental.pallas import tpu as pltpu

F32 = jnp.float32
BF16 = jnp.bfloat16

D_MODEL = 1024
DEPTH = 2
GRID_W = 64
EPS = 1e-6
N_BRANCH = 4
A_HEADS, A_DH, A_CHUNK = 4, 128, 128
A_WIDTH = A_HEADS * A_DH
B_HEADS, B_KV, B_DH = 8, 2, 64
B_WIDTH = B_HEADS * B_DH
ROPE_THETA = 10000.0
C_HEADS, C_DH, C_WIN_R, C_WIN_C = 8, 64, 8, 16
C_WIDTH = C_HEADS * C_DH
D_HEADS, D_DH = 4, 64
D_DV = 2 * D_DH
D_WIDTH = D_HEADS * D_DV
D_FF = ((8 * D_MODEL + 3 * 256 - 1) // (3 * 256)) * 256
FF_SPLITS = ((0, 1536), (1536, D_FF))

_SIZES = (A_WIDTH, A_WIDTH, A_WIDTH, A_WIDTH, 4 * A_HEADS, B_WIDTH, B_KV * B_DH, B_KV * B_DH,
          C_WIDTH, C_WIDTH, C_WIDTH, 2 * D_HEADS * D_DH, 2 * D_HEADS * D_DH, D_WIDTH, N_BRANCH * D_MODEL)
_NAMES = ("aq", "ak", "av", "ao", "ag", "bq", "bk", "bv", "cq", "ck", "cv", "dq", "dk", "dv", "gl")
_OFF = {}
_o = 0
for _n, _s in zip(_NAMES, _SIZES):
    _OFF[_n] = (_o, _o + _s)
    _o += _s

_MAIN_ORDER = ("gl", "aq", "ak", "ao", "cq", "ck", "cv", "dk")
_MAIN_OFF = {}
_o = 0
for _n in _MAIN_ORDER:
    _MAIN_OFF[_n] = _o
    _o += _OFF[_n][1] - _OFF[_n][0]
MAIN_N = _o

KC = 512
VMEM_LIMIT = 56 * 1024 * 1024


def _cparams(sem, vmem=VMEM_LIMIT):
    return pltpu.CompilerParams(dimension_semantics=sem, vmem_limit_bytes=vmem)


def _sigmoid(x):
    return 1.0 / (1.0 + jnp.exp(-x))


def _aligned(x, m):
    return x if isinstance(x, int) else pl.multiple_of(x, m)


def _proj_kernel(x_ref, g_ref, w_ref, wg_ref, o_ref, og_ref, h_ref):
    @pl.when(pl.program_id(1) == 0)
    def _():
        x = x_ref[...]
        ms = jnp.mean(x * x, axis=-1, keepdims=True)
        hb = (x * lax.rsqrt(ms + EPS) * g_ref[...]).astype(BF16)
        h_ref[...] = hb
        og_ref[...] = jnp.dot(hb, wg_ref[...], preferred_element_type=F32)

    o_ref[...] = jnp.dot(h_ref[...], w_ref[...], preferred_element_type=F32).astype(o_ref.dtype)


def _proj(x2, g, w_main, w_gate):
    t = x2.shape[0]
    tm = min(2048, t)
    n = w_main.shape[1]
    tn = 768 if n % 768 == 0 else 1024
    ng = w_gate.shape[1]
    return pl.pallas_call(
        _proj_kernel,
        out_shape=(jax.ShapeDtypeStruct((t, n), BF16),
                   jax.ShapeDtypeStruct((t, ng), F32),
                   jax.ShapeDtypeStruct((t, D_MODEL), BF16)),
        grid=(t // tm, n // tn),
        in_specs=[pl.BlockSpec((tm, D_MODEL), lambda i, j: (i, 0)),
                  pl.BlockSpec((1, D_MODEL), lambda i, j: (0, 0)),
                  pl.BlockSpec((D_MODEL, tn), lambda i, j: (0, j)),
                  pl.BlockSpec((D_MODEL, ng), lambda i, j: (0, 0))],
        out_specs=(pl.BlockSpec((tm, tn), lambda i, j: (i, j)),
                   pl.BlockSpec((tm, ng), lambda i, j: (i, 0)),
                   pl.BlockSpec((tm, D_MODEL), lambda i, j: (i, 0))),
        compiler_params=_cparams(("parallel", "arbitrary")),
        name="proj_main",
    )(x2, g, w_main, w_gate)


def _projt_kernel(h_ref, wt_ref, o_ref, *, nchunk, chunk):
    for c in range(nchunk):
        hc = h_ref[c * chunk:(c + 1) * chunk, :]
        o_ref[c] = lax.dot_general(wt_ref[...], hc, (((1,), (1,)), ((), ())),
                                   preferred_element_type=F32).astype(o_ref.dtype)


def _projt(h, wt, out_dtype, name, chunk=KC):
    t = h.shape[0]
    n = wt.shape[0]
    tm = min(2048, t)
    tn = n if n <= 768 else (768 if n % 768 == 0 else 512)
    nchunk = tm // chunk
    return pl.pallas_call(
        functools.partial(_projt_kernel, nchunk=nchunk, chunk=chunk),
        out_shape=jax.ShapeDtypeStruct((t // chunk, n, chunk), out_dtype),
        grid=(t // tm, n // tn),
        in_specs=[pl.BlockSpec((tm, D_MODEL), lambda i, j: (i, 0)),
                  pl.BlockSpec((tn, D_MODEL), lambda i, j: (j, 0))],
        out_specs=pl.BlockSpec((nchunk, tn, chunk), lambda i, j: (i, j, 0)),
        compiler_params=_cparams(("parallel", "arbitrary")),
        name=name,
    )(h, wt)


def _log_sigmoid(x):
    return jnp.minimum(x, 0.0) - jnp.log1p(jnp.exp(-jnp.abs(x)))


def _tri_dot(mat, x):
    hi = x.astype(BF16)
    r1 = x - hi.astype(F32)
    mid = r1.astype(BF16)
    lo = (r1 - mid.astype(F32)).astype(BF16)
    return (jnp.dot(mat, hi, preferred_element_type=F32)
            + jnp.dot(mat, mid, preferred_element_type=F32)
            + jnp.dot(mat, lo, preferred_element_type=F32))


def _split3(x):
    hi = x.astype(BF16)
    r1 = x - hi.astype(F32)
    mid = r1.astype(BF16)
    return hi, mid, (r1 - mid.astype(F32)).astype(BF16)


def _tri_dot_r(x, mat):
    return sum(jnp.dot(piece, mat, preferred_element_type=F32) for piece in _split3(x))


NROWS = 8


def _mlstm_kernel(q_ref, k_ref, vt_ref, o_ref, g_ref, gt_ref, gb_ref, gbc_ref, cw_ref, ng_ref, y_ref,
                  qs_ref, ks_ref, hs_ref, st_ref, m_ref, *, seq):
    L = A_CHUNK
    nc = seq // L
    hw = 2 * A_DH

    rowi = lax.broadcasted_iota(jnp.int32, (L, hw), 0)

    def conv_body(c, carry):
        r0 = pl.multiple_of(c * L, L)
        pstart = pl.multiple_of(jnp.maximum(r0 - 16, 0), 16)
        nstart = pl.multiple_of(jnp.minimum(r0 + L, seq - 16), 16)
        has_prev = jnp.where(c > 0, 1.0, 0.0)
        has_next = jnp.where(c < nc - 1, 1.0, 0.0)
        for src, dst, woff, scale in ((q_ref, qs_ref, 0, 1.0), (k_ref, ks_ref, hw, A_DH ** -0.5)):
            xc = src[pl.ds(r0, L), :].astype(F32)
            prev = src[pl.ds(pstart, 16), :].astype(F32)[15:16] * has_prev
            nxt = src[pl.ds(nstart, 16), :].astype(F32)[0:1] * has_next
            xp = jnp.where(rowi == 0, prev, pltpu.roll(xc, 1, 0))
            xn = jnp.where(rowi == L - 1, nxt, pltpu.roll(xc, L - 1, 0))
            w = cw_ref[:, woff:woff + hw]
            y = xp * w[0:1] + xc * w[1:2] + xn * w[2:3]
            y = y * _sigmoid(y)
            dst[pl.ds(r0, L), :] = (y * scale).astype(BF16)
        return carry

    lax.fori_loop(0, nc, conv_body, 0)

    st_ref[...] = jnp.zeros(st_ref.shape, F32)
    m_ref[...] = jnp.zeros(m_ref.shape, F32)

    ti = lax.broadcasted_iota(jnp.int32, (L, L), 0)
    tj = lax.broadcasted_iota(jnp.int32, (L, L), 1)
    lower = tj <= ti
    upper = tj >= ti
    lmat = jnp.where(lower, 1.0, 0.0).astype(BF16)
    umat = jnp.where(upper, 1.0, 0.0).astype(BF16)
    nt = (((1,), (1,)), ((), ()))

    def step(cf, cb):
        chains = []
        for d, c in ((0, cf), (1, cb)):
            r0 = pl.multiple_of(c * L, L)
            gcol = g_ref[pl.ds(r0, L), :] + gb_ref[...]
            bcol_all = _tri_dot(lmat if d == 0 else umat, _log_sigmoid(gcol))
            grow = gt_ref[c] + gbc_ref[...]
            brow_all = _tri_dot_r(_log_sigmoid(grow), umat if d == 0 else lmat)
            for hh in range(2):
                il = 4 * d + hh
                fl = il + 2
                ch = dict(d=d, idx=2 * d + hh, c=c)
                ch["acol"] = gcol[:, il:il + 1] - bcol_all[:, fl:fl + 1]
                ch["irow"] = grow[il:il + 1, :]
                ch["brow"] = brow_all[fl:fl + 1, :]
                ch["q"] = qs_ref[pl.ds(r0, L), hh * A_DH:(hh + 1) * A_DH]
                ch["k"] = ks_ref[pl.ds(r0, L), hh * A_DH:(hh + 1) * A_DH]
                ch["vt"] = vt_ref[c, hh * A_DH:(hh + 1) * A_DH, :]
                ch["state"] = st_ref[ch["idx"]]
                ch["st"] = lax.dot_general(ch["k"], ch["q"], nt, preferred_element_type=F32)
                ch["it"] = lax.dot_general(ch["state"].astype(BF16), ch["q"], nt,
                                           preferred_element_type=F32)
                chains.append(ch)
        for ch in chains:
            d, brow = ch["d"], ch["brow"]
            mask = upper if d == 0 else lower
            m_prev = m_ref[ch["idx"]][:, 0:1]
            dmt = jnp.where(mask, brow + ch["acol"], -jnp.inf)
            inter = brow + m_prev
            mt = jnp.maximum(inter, jnp.max(dmt, axis=0, keepdims=True))
            ch["w_inter"] = jnp.exp(inter - mt)
            ch["floor"] = jnp.exp(-mt)
            sqk = ch["st"] * jnp.exp(dmt - mt)
            ch["sqk_sum"] = jnp.sum(sqk, axis=0, keepdims=True)
            ch["sqk"] = sqk.astype(BF16)
            bl = brow[:, L - 1:L] if d == 0 else brow[:, 0:1]
            gvec = bl - brow + ch["irow"]
            m_new = jnp.maximum(bl + m_prev, jnp.max(gvec, axis=1, keepdims=True))
            ch["wc"] = jnp.exp(bl + m_prev - m_new)
            ws = jnp.exp(gvec - m_new)
            ch["lhs"] = jnp.concatenate([ch["vt"].astype(F32) * ws, jnp.broadcast_to(ws, (NROWS, L))],
                                        axis=0).astype(BF16)
            m_ref[ch["idx"]] = jnp.broadcast_to(m_new, (1, 128))
        for ch in chains:
            ch["pv"] = jnp.dot(ch["vt"], ch["sqk"], preferred_element_type=F32)
            ch["upd"] = jnp.dot(ch["lhs"], ch["k"], preferred_element_type=F32)
        outs = {0: [], 1: []}
        for ch in chains:
            num = ch["w_inter"] * ch["it"][:A_DH] + ch["pv"]
            den = ch["w_inter"] * ch["it"][A_DH:A_DH + 1] + ch["sqk_sum"]
            outs[ch["d"]].append(num / jnp.maximum(jnp.abs(den), ch["floor"]))
            st_ref[ch["idx"]] = ch["wc"] * ch["state"] + ch["upd"]
        return outs

    def finalize(tots, r0):
        parts = []
        for tot in tots:
            mu = jnp.mean(tot, axis=0, keepdims=True)
            cen = tot - mu
            var = jnp.mean(cen * cen, axis=0, keepdims=True)
            parts.append((cen * lax.rsqrt(var + EPS)).T)
        hn = jnp.concatenate(parts, axis=1) * ng_ref[...]
        y_ref[pl.ds(r0, L), :] = (hn * _sigmoid(o_ref[pl.ds(r0, L), :].astype(F32))).astype(y_ref.dtype)

    def first_half(j, carry):
        outs = step(j, nc - 1 - j)
        for d, c in ((0, j), (1, nc - 1 - j)):
            for hh in range(2):
                hs_ref[c, hh] = outs[d][hh]
        return carry

    def second_half(j, carry):
        outs = step(j, nc - 1 - j)
        for d, c in ((0, j), (1, nc - 1 - j)):
            finalize([hs_ref[c, hh] + outs[d][hh] for hh in range(2)], pl.multiple_of(c * L, L))
        return carry

    lax.fori_loop(0, nc // 2, first_half, 0)
    lax.fori_loop(nc // 2, nc, second_half, 0)


def _mlstm(main, vt, gates, gates_t, gate_bias, gate_bias_col, conv_w, norm_g, bsz, seq):
    t = bsz * seq
    hw = 2 * A_DH
    nc = seq // A_CHUNK
    assert nc % 2 == 0

    def col(name):
        base = _MAIN_OFF[name] // hw
        return pl.BlockSpec((seq, hw), lambda b, p: (b, base + p))

    return pl.pallas_call(
        functools.partial(_mlstm_kernel, seq=seq),
        out_shape=jax.ShapeDtypeStruct((t, A_WIDTH), BF16),
        grid=(bsz, 2),
        in_specs=[col("aq"), col("ak"),
                  pl.BlockSpec((nc, hw, A_CHUNK), lambda b, p: (b, p, 0)),
                  col("ao"),
                  pl.BlockSpec((seq, 128), lambda b, p: (b, p)),
                  pl.BlockSpec((nc, 8, A_CHUNK), lambda b, p: (b, p, 0)),
                  pl.BlockSpec((1, 128), lambda b, p: (0, p)),
                  pl.BlockSpec((8, 1), lambda b, p: (p, 0)),
                  pl.BlockSpec((None, 3, 2 * hw), lambda b, p: (p, 0, 0)),
                  pl.BlockSpec((1, hw), lambda b, p: (0, p))],
        out_specs=pl.BlockSpec((seq, hw), lambda b, p: (b, p)),
        scratch_shapes=[pltpu.VMEM((seq, hw), BF16), pltpu.VMEM((seq, hw), BF16),
                        pltpu.VMEM((nc, 2, A_DH, A_CHUNK), F32),
                        pltpu.VMEM((4, A_DH + NROWS, A_DH), F32),
                        pltpu.VMEM((4, 1, 128), F32)],
        compiler_params=_cparams(("parallel", "arbitrary")),
        name="mlstm",
    )(main, main, vt, main, gates, gates_t, gate_bias, gate_bias_col, conv_w, norm_g)


LOG2E = 1.4426950408889634


def _stash_scores(s, s_ref, mx_ref, idx):
    tk, tq = s.shape
    s_ref[idx] = s
    mx_ref[idx] = jnp.max(s.reshape(tk // 8, 8, tq), axis=0)


ONES_ROWS = 8


def _with_ones(vc):
    return jnp.concatenate([vc, jnp.ones((ONES_ROWS, vc.shape[1]), vc.dtype)], axis=0)


def _flash_update(s_ref, mx_ref, vc1, m_ref, acc_ref, idx):
    m_old = m_ref[idx]
    m_new = jnp.maximum(m_old, jnp.max(mx_ref[idx], axis=0, keepdims=True))
    alpha = jnp.exp2(m_old - m_new)
    p = jnp.exp2(s_ref[idx] - m_new)
    acc_ref[idx] = alpha * acc_ref[idx] + jnp.dot(vc1, p.astype(BF16), preferred_element_type=F32)
    m_ref[idx] = m_new


def _normalised(acc_ref, idx, dv):
    return acc_ref[idx, :dv] / acc_ref[idx, dv:dv + 1]


HK = KC // 2


def _pipelined_chunks(nkc, scores, consume):
    scores(0, 0)

    def body(c, carry):
        scores(c, 1)
        consume(c, 0)
        scores(c + 1, 0)
        consume(c, 1)
        return carry

    lax.fori_loop(0, nkc - 1, body, 0)
    scores(nkc - 1, 1)
    consume(nkc - 1, 0)
    consume(nkc - 1, 1)


def _gqa_prep_kernel(fm_ref, cos_ref, sin_ref, qg_ref, kg_ref, q_ref, k_ref, v_ref):
    x = fm_ref[0]
    cos = cos_ref[...]
    sin = sin_ref[...]

    def norm_rope(xh, g):
        ms = jnp.mean(xh * xh, axis=0, keepdims=True)
        xn = xh * lax.rsqrt(ms + EPS) * g
        partner = jnp.concatenate([xn[16:32], xn[0:16], xn[48:64], xn[32:48]], axis=0)
        return xn * cos + partner * sin

    for h in range(B_HEADS):
        qh = norm_rope(x[h * B_DH:(h + 1) * B_DH], qg_ref[...]) * (B_DH ** -0.5 * LOG2E)
        q_ref[0, h * B_DH:(h + 1) * B_DH, :] = qh.astype(BF16)
    ks = [norm_rope(x[B_WIDTH + g * B_DH:B_WIDTH + (g + 1) * B_DH], kg_ref[...]) for g in range(B_KV)]
    k_ref[...] = jnp.concatenate(ks, axis=0).T.astype(BF16)
    v_ref[0] = x[B_WIDTH + B_KV * B_DH:].astype(BF16)


def _gqa_prep(fm_b, cos, sin, qg, kg, seq):
    nchunks = fm_b.shape[0]
    t = nchunks * KC
    spc = seq // KC
    kvw = B_KV * B_DH
    return pl.pallas_call(
        _gqa_prep_kernel,
        out_shape=(jax.ShapeDtypeStruct((nchunks, B_WIDTH, KC), BF16),
                   jax.ShapeDtypeStruct((t, kvw), BF16),
                   jax.ShapeDtypeStruct((nchunks, kvw, KC), BF16)),
        grid=(nchunks,),
        in_specs=[pl.BlockSpec((1, B_WIDTH + 2 * kvw, KC), lambda i: (i, 0, 0)),
                  pl.BlockSpec((B_DH, KC), lambda i: (0, i % spc)),
                  pl.BlockSpec((B_DH, KC), lambda i: (0, i % spc)),
                  pl.BlockSpec((B_DH, 1), lambda i: (0, 0)),
                  pl.BlockSpec((B_DH, 1), lambda i: (0, 0))],
        out_specs=(pl.BlockSpec((1, B_WIDTH, KC), lambda i: (i, 0, 0)),
                   pl.BlockSpec((KC, kvw), lambda i: (i, 0)),
                   pl.BlockSpec((1, kvw, KC), lambda i: (i, 0, 0))),
        compiler_params=_cparams(("parallel",)),
        name="gqa_prep",
    )(fm_b, cos, sin, qg, kg)


def _gqa_attn_kernel(q_ref, k_ref, v_ref, o_ref, qpad_ref, s0_ref, s1_ref, mx0_ref, mx1_ref,
                     m_ref, acc_ref, *, nkc, tq):
    g = pl.program_id(1)
    rep = B_HEADS // B_KV
    row = lax.broadcasted_iota(jnp.int32, (2 * B_DH, tq), 0)
    sel = (row // B_DH) == g
    for r in range(rep):
        qh = q_ref[0, r * B_DH:(r + 1) * B_DH, :].astype(F32)
        qpad_ref[r] = jnp.where(sel, jnp.concatenate([qh, qh], axis=0), 0.0).astype(BF16)
    m_ref[...] = jnp.full(m_ref.shape, -jnp.inf, F32)
    acc_ref[...] = jnp.zeros(acc_ref.shape, F32)
    slots = ((s0_ref, mx0_ref), (s1_ref, mx1_ref))

    def scores(c, half):
        s_ref, mx_ref = slots[half]
        kc = k_ref[pl.ds(_aligned(c * KC + half * HK, HK), HK), :]
        for r in range(rep):
            _stash_scores(jnp.dot(kc, qpad_ref[r], preferred_element_type=F32), s_ref, mx_ref, r)

    def consume(c, half):
        s_ref, mx_ref = slots[half]
        vc1 = _with_ones(v_ref[c, :, half * HK:(half + 1) * HK])
        for r in range(rep):
            _flash_update(s_ref, mx_ref, vc1, m_ref, acc_ref, r)

    _pipelined_chunks(nkc, scores, consume)
    outs = [_normalised(acc_ref, r, B_DH) for r in range(rep)]
    o_ref[...] = jnp.concatenate(outs, axis=0).T.astype(o_ref.dtype)


def _gqa_attn(qt, k_tok, v_fm, bsz, seq):
    t = bsz * seq
    tq = KC
    nq = seq // tq
    nkc = seq // KC
    rep = B_HEADS // B_KV
    gw = rep * B_DH
    return pl.pallas_call(
        functools.partial(_gqa_attn_kernel, nkc=nkc, tq=tq),
        out_shape=jax.ShapeDtypeStruct((t, B_WIDTH), BF16),
        grid=(bsz, B_KV, nq),
        in_specs=[pl.BlockSpec((1, gw, tq), lambda b, g, i: (b * nkc + i, g, 0)),
                  pl.BlockSpec((seq, B_KV * B_DH), lambda b, g, i: (b, 0)),
                  pl.BlockSpec((nkc, B_DH, KC), lambda b, g, i: (b, g, 0))],
        out_specs=pl.BlockSpec((tq, gw), lambda b, g, i: (b * nq + i, g)),
        scratch_shapes=[pltpu.VMEM((rep, 2 * B_DH, tq), BF16),
                        pltpu.VMEM((rep, HK, tq), F32), pltpu.VMEM((rep, HK, tq), F32),
                        pltpu.VMEM((rep, 8, tq), F32), pltpu.VMEM((rep, 8, tq), F32),
                        pltpu.VMEM((rep, 1, tq), F32),
                        pltpu.VMEM((rep, B_DH + ONES_ROWS, tq), F32)],
        compiler_params=_cparams(("parallel", "parallel", "arbitrary")),
        name="gqa_attn",
    )(qt, k_tok, v_fm)


NAT_ROWS = 2


def _natten_kernel(q_ref, k_ref, v_ref, *rest, rows):
    bias_refs, o_ref = rest[:NAT_ROWS], rest[NAT_ROWS]
    nkeys = C_WIN_R * GRID_W
    npair = C_HEADS // 2
    own = (lax.broadcasted_iota(jnp.int32, (2 * GRID_W, 2 * C_DH), 0) // GRID_W
           == lax.broadcasted_iota(jnp.int32, (2 * GRID_W, 2 * C_DH), 1) // C_DH)
    starts = []
    for rr in range(NAT_ROWS):
        r = pl.program_id(1) * NAT_ROWS + rr
        rs = jnp.clip(r - C_WIN_R // 2, 0, rows - C_WIN_R)
        starts.append(pl.multiple_of(rs * GRID_W, GRID_W))
    units = [(rr, p) for rr in range(NAT_ROWS) for p in range(npair)]
    scores = []
    for rr, p in units:
        kp = k_ref[pl.ds(starts[rr], nkeys), p * 128:(p + 1) * 128]
        qp = q_ref[rr * GRID_W:(rr + 1) * GRID_W, p * 128:(p + 1) * 128].astype(F32)
        qm = jnp.where(own, jnp.concatenate([qp, qp], axis=0), 0.0).astype(BF16)
        s = lax.dot_general(qm, kp, (((1,), (1,)), ((), ())), preferred_element_type=F32)
        scores.append(s + bias_refs[rr][0, 2 * p:2 * p + 2].reshape(2 * GRID_W, nkeys))
    probs = []
    for s in scores:
        e = jnp.exp(s - jnp.max(s, axis=1, keepdims=True))
        probs.append((e.astype(BF16), jnp.sum(e, axis=1, keepdims=True)))
    outs = [[] for _ in range(NAT_ROWS)]
    for (rr, p), (e, l) in zip(units, probs):
        vp = v_ref[pl.ds(starts[rr], nkeys), p * 128:(p + 1) * 128]
        o = jnp.where(own, jnp.dot(e, vp, preferred_element_type=F32) / l, 0.0)
        outs[rr].append(o[:GRID_W] + o[GRID_W:])
    o_ref[...] = jnp.concatenate([jnp.concatenate(o, axis=1) for o in outs], axis=0).astype(o_ref.dtype)


def _natten_bias_table(rpb, rows):
    c = np.arange(GRID_W)[:, None]
    kc = np.arange(GRID_W)[None, :]
    cs = np.clip(c - C_WIN_C // 2, 0, GRID_W - C_WIN_C)
    valid = (kc >= cs) & (kc < cs + C_WIN_C)
    nd = 2 * C_WIN_C - 1
    onehot = (kc - c + (C_WIN_C - 1))[None] == np.arange(nd)[:, None, None]
    toe = jnp.sum(jnp.where(onehot[None, None], rpb.astype(F32)[:, :, :, None, None], 0.0), axis=2)
    toe = jnp.where(valid[None, None], toe, -1e30)
    tbl = jnp.stack([toe[:, C_WIN_R - 1 - v:2 * C_WIN_R - 1 - v] for v in range(C_WIN_R)], axis=0)
    return jnp.swapaxes(tbl, 2, 3).reshape(C_WIN_R, C_HEADS, GRID_W, C_WIN_R * GRID_W)


def _natten(main, bias_tbl, bsz, seq):
    t = bsz * seq
    rows = seq // GRID_W
    assert rows >= C_WIN_R
    half = C_WIN_R // 2

    def col(name):
        return _MAIN_OFF[name] // C_WIDTH

    cq, ck, cv = col("cq"), col("ck"), col("cv")
    assert rows % NAT_ROWS == 0
    steps = rows // NAT_ROWS

    def bias_spec(rr):
        def variant(b, i):
            r = i * NAT_ROWS + rr
            return (r - jnp.clip(r - half, 0, rows - C_WIN_R), 0, 0, 0)
        return pl.BlockSpec((1, C_HEADS, GRID_W, C_WIN_R * GRID_W), variant)

    return pl.pallas_call(
        functools.partial(_natten_kernel, rows=rows),
        out_shape=jax.ShapeDtypeStruct((t, C_WIDTH), BF16),
        grid=(bsz, steps),
        in_specs=[pl.BlockSpec((NAT_ROWS * GRID_W, C_WIDTH), lambda b, i: (b * steps + i, cq)),
                  pl.BlockSpec((seq, C_WIDTH), lambda b, i: (b, ck)),
                  pl.BlockSpec((seq, C_WIDTH), lambda b, i: (b, cv))]
                 + [bias_spec(rr) for rr in range(NAT_ROWS)],
        out_specs=pl.BlockSpec((NAT_ROWS * GRID_W, C_WIDTH), lambda b, i: (b * steps + i, 0)),
        compiler_params=_cparams(("parallel", "arbitrary")),
        name="natten",
    )(main, main, main, *([bias_tbl] * NAT_ROWS))


def _diff_attn_kernel(slopes_ref, q1_ref, q2_ref, k1_ref, k2_ref, v_ref, lq1_ref, lk1_ref, lq2_ref, lk2_ref,
                      sg_ref, o_ref, qpad_ref, e_ref, bias_ref, s0_ref, s1_ref, mx0_ref, mx1_ref, m_ref, acc_ref,
                      *, nkc, tq, lambda_init):
    pair = pl.program_id(1)
    qi = pl.program_id(2)
    row = lax.broadcasted_iota(jnp.int32, (2 * D_DH, tq), 0)
    for c, qr in enumerate((q1_ref, q2_ref)):
        qf = qr[0].astype(F32)
        for hh in range(2):
            qpad_ref[2 * c + hh] = jnp.where((row // D_DH) == hh, qf, 0.0).astype(BF16)
    m_ref[...] = jnp.full(m_ref.shape, -jnp.inf, F32)
    acc_ref[...] = jnp.zeros(acc_ref.shape, F32)
    kk = lax.broadcasted_iota(jnp.int32, (HK, tq), 0)
    qq = lax.broadcasted_iota(jnp.int32, (HK, tq), 1)
    e_ref[...] = (kk - qq).astype(F32)
    slots = ((s0_ref, mx0_ref), (s1_ref, mx1_ref))
    krefs = (k1_ref, k2_ref)

    def scores(ci, half):
        s_ref, mx_ref = slots[half]
        off = (ci * KC + half * HK - qi * tq).astype(F32)
        dist = jnp.abs(e_ref[...] + off)
        r0 = _aligned(ci * KC + half * HK, HK)
        for hh in range(2):
            bias_ref[half, hh] = dist * (-slopes_ref[2 * pair + hh])
        for hh in range(2):
            for c in range(2):
                s = jnp.dot(krefs[c][pl.ds(r0, HK), :], qpad_ref[2 * c + hh], preferred_element_type=F32)
                _stash_scores(s + bias_ref[half, hh], s_ref, mx_ref, 2 * c + hh)

    def consume(ci, half):
        s_ref, mx_ref = slots[half]
        for hh in range(2):
            vc1 = _with_ones(v_ref[ci, hh * D_DV:(hh + 1) * D_DV, half * HK:(half + 1) * HK])
            for c in range(2):
                _flash_update(s_ref, mx_ref, vc1, m_ref, acc_ref, 2 * c + hh)

    _pipelined_chunks(nkc, scores, consume)
    lam = (jnp.exp(jnp.sum(lq1_ref[...] * lk1_ref[...], axis=1, keepdims=True))
           - jnp.exp(jnp.sum(lq2_ref[...] * lk2_ref[...], axis=1, keepdims=True)) + lambda_init)
    outs = []
    for hh in range(2):
        o = _normalised(acc_ref, hh, D_DV) - lam * _normalised(acc_ref, 2 + hh, D_DV)
        ms = jnp.mean(o * o, axis=0, keepdims=True)
        outs.append(o * lax.rsqrt(ms + EPS) * sg_ref[...] * (1.0 - lambda_init))
    o_ref[...] = jnp.concatenate(outs, axis=0).T.astype(o_ref.dtype)


def _diff_attn(fm_d, main, lq1, lk1, lq2, lk2, subln_g, lambda_init, bsz, seq):
    t = bsz * seq
    tq = KC
    nq = seq // tq
    nkc = seq // KC
    pw = 2 * D_DH
    vw = 2 * D_DV
    dk0 = _MAIN_OFF["dk"] // pw
    vrow0 = (2 * D_HEADS * D_DH) // vw
    slopes = jnp.asarray([LOG2E * 2.0 ** (-8.0 * (h + 1) / D_HEADS) for h in range(D_HEADS)], F32)

    def qspec(c):
        return pl.BlockSpec((1, pw, tq), lambda b, p, i: (b * nkc + i, 2 * c + p, 0))

    def kspec(c):
        return pl.BlockSpec((seq, pw), lambda b, p, i: (b, dk0 + 2 * c + p))

    vec = pl.BlockSpec((1, D_DH), lambda b, p, i: (0, 0))
    return pl.pallas_call(
        functools.partial(_diff_attn_kernel, nkc=nkc, tq=tq, lambda_init=lambda_init),
        out_shape=jax.ShapeDtypeStruct((t, D_WIDTH), BF16),
        grid=(bsz, D_HEADS // 2, nq),
        in_specs=[pl.BlockSpec(memory_space=pltpu.SMEM),
                  qspec(0), qspec(1), kspec(0), kspec(1),
                  pl.BlockSpec((nkc, vw, KC), lambda b, p, i: (b, vrow0 + p, 0)),
                  vec, vec, vec, vec,
                  pl.BlockSpec((D_DV, 1), lambda b, p, i: (0, 0))],
        out_specs=pl.BlockSpec((tq, vw), lambda b, p, i: (b * nq + i, p)),
        scratch_shapes=[pltpu.VMEM((4, 2 * D_DH, tq), BF16),
                        pltpu.VMEM((HK, tq), F32),
                        pltpu.VMEM((2, 2, HK, tq), F32),
                        pltpu.VMEM((4, HK, tq), F32), pltpu.VMEM((4, HK, tq), F32),
                        pltpu.VMEM((4, 8, tq), F32), pltpu.VMEM((4, 8, tq), F32),
                        pltpu.VMEM((4, 1, tq), F32),
                        pltpu.VMEM((4, D_DV + ONES_ROWS, tq), F32)],
        compiler_params=_cparams(("parallel", "parallel", "arbitrary")),
        name="diff_attn",
    )(slopes, fm_d, fm_d, main, main, fm_d, lq1, lk1, lq2, lk2, subln_g)


def _merge_kernel(x_ref, gl_ref, ya_ref, yb_ref, yc_ref, yd_ref, wup_ref, wout_ref, o_ref):
    merged = None
    for g, y_ref in enumerate((ya_ref, yb_ref, yc_ref, yd_ref)):
        u = jnp.dot(y_ref[...], wup_ref[g], preferred_element_type=F32)
        gate = gl_ref[:, g * D_MODEL:(g + 1) * D_MODEL].astype(F32)
        term = _sigmoid(gate) * u
        merged = term if merged is None else merged + term
    o_ref[...] = x_ref[...] + jnp.dot(merged.astype(BF16), wout_ref[...], preferred_element_type=F32)


def _merge(x2, main, ya, yb, yc, yd, w_up, w_out):
    t = x2.shape[0]
    tm = min(512, t)
    ytile = pl.BlockSpec((tm, 512), lambda i: (i, 0))
    return pl.pallas_call(
        _merge_kernel,
        out_shape=jax.ShapeDtypeStruct((t, D_MODEL), F32),
        grid=(t // tm,),
        in_specs=[pl.BlockSpec((tm, D_MODEL), lambda i: (i, 0)),
                  pl.BlockSpec((tm, N_BRANCH * D_MODEL), lambda i: (i, 0)),
                  ytile, ytile, ytile, ytile,
                  pl.BlockSpec((N_BRANCH, 512, D_MODEL), lambda i: (0, 0, 0)),
                  pl.BlockSpec((D_MODEL, D_MODEL), lambda i: (0, 0))],
        out_specs=pl.BlockSpec((tm, D_MODEL), lambda i: (i, 0)),
        compiler_params=_cparams(("parallel",)),
        name="merge",
    )(x2, main, ya, yb, yc, yd, w_up, w_out)


def _ffn_kernel(x_ref, g_ref, wg_ref, wu_ref, wd_ref, fg_ref, o_ref, *, final):
    x = x_ref[...]
    ms = jnp.mean(x * x, axis=-1, keepdims=True)
    h = (x * lax.rsqrt(ms + EPS) * g_ref[...]).astype(BF16)
    acc = x
    for a, b in FF_SPLITS:
        gt = jnp.dot(h, wg_ref[:, a:b], preferred_element_type=F32)
        up = jnp.dot(h, wu_ref[:, a:b], preferred_element_type=F32)
        act = (gt * _sigmoid(gt)) * up
        acc = acc + jnp.dot(act.astype(BF16), wd_ref[a:b, :], preferred_element_type=F32)
    if final:
        ms2 = jnp.mean(acc * acc, axis=-1, keepdims=True)
        acc = acc * lax.rsqrt(ms2 + EPS) * fg_ref[...]
    o_ref[...] = acc


def _ffn(x2, g, wg, wu, wd, fg, final):
    t = x2.shape[0]
    tm = min(512, t)
    return pl.pallas_call(
        functools.partial(_ffn_kernel, final=final),
        out_shape=jax.ShapeDtypeStruct((t, D_MODEL), F32),
        grid=(t // tm,),
        in_specs=[pl.BlockSpec((tm, D_MODEL), lambda i: (i, 0)),
                  pl.BlockSpec((1, D_MODEL), lambda i: (0, 0)),
                  pl.BlockSpec((D_MODEL, D_FF), lambda i: (0, 0)),
                  pl.BlockSpec((D_MODEL, D_FF), lambda i: (0, 0)),
                  pl.BlockSpec((D_FF, D_MODEL), lambda i: (0, 0)),
                  pl.BlockSpec((1, D_MODEL), lambda i: (0, 0))],
        out_specs=pl.BlockSpec((tm, D_MODEL), lambda i: (i, 0)),
        compiler_params=_cparams(("parallel",)),
        name="ffn",
    )(x2, g, wg, wu, wd, fg)


def _rope_tables(seq):
    tpos = jnp.arange(seq)
    row = (tpos // GRID_W).astype(F32)
    colp = (tpos % GRID_W).astype(F32)
    n_freq = B_DH // 4
    inv = ROPE_THETA ** (-jnp.arange(n_freq, dtype=F32) / n_freq)
    ar = (row[:, None] * inv).T
    ac = (colp[:, None] * inv).T
    cos = jnp.concatenate([jnp.cos(ar), jnp.cos(ar), jnp.cos(ac), jnp.cos(ac)], axis=0)
    sin = jnp.concatenate([-jnp.sin(ar), jnp.sin(ar), -jnp.sin(ac), jnp.sin(ac)], axis=0)
    return cos.astype(F32), sin.astype(F32)


def _w_cols(w_in, name, scale=None):
    a, b = _OFF[name]
    w = w_in[:, a:b]
    return w if scale is None else w * scale


def _layer(x2, l, bsz, seq, cos, sin, norm1_g, w_in, a_conv_w, a_gate_bias, a_norm_g, b_qnorm_g, b_knorm_g,
           c_rpb, d_lq1, d_lk1, d_lq2, d_lk2, d_subln_g, w_up_a, w_up_b, w_up_c, w_up_d, w_out,
           norm2_g, w_ffn_gate, w_ffn_up, w_ffn_down, final_g, final):
    scales = {"cq": C_DH ** -0.5, "dq": D_DH ** -0.5 * LOG2E}
    w_main = jnp.concatenate([_w_cols(w_in, n, scales.get(n)) for n in _MAIN_ORDER], axis=1).astype(BF16)
    ga, _ = _OFF["ag"]
    gate_cols = [ga + ty * A_HEADS + 2 * p + hh for p in range(2) for ty in range(4) for hh in range(2)]
    wg16 = w_in[:, jnp.asarray(gate_cols)]
    w_gate = jnp.pad(wg16.reshape(D_MODEL, 2, 8), ((0, 0), (0, 0), (0, 120))).reshape(D_MODEL, 256).astype(BF16)
    gb = a_gate_bias[jnp.asarray([c - ga for c in gate_cols])].reshape(2, 8)
    gate_bias = jnp.pad(gb, ((0, 0), (0, 120))).reshape(1, 256).astype(F32)
    gate_bias_col = gb.reshape(16, 1).astype(F32)
    wt_av = _w_cols(w_in, "av").T.astype(BF16)
    wt_b = jnp.concatenate([_w_cols(w_in, n) for n in ("bq", "bk", "bv")], axis=1).T.astype(BF16)
    wt_d = jnp.concatenate([_w_cols(w_in, "dq", scales["dq"]), _w_cols(w_in, "dv")], axis=1).T.astype(BF16)
    cw = a_conv_w.astype(F32)
    conv_w = jnp.stack([jnp.concatenate([cw[:, p * 256:(p + 1) * 256],
                                         cw[:, A_WIDTH + p * 256:A_WIDTH + (p + 1) * 256]], axis=1)
                        for p in range(2)], axis=0)

    main, gates, h = _proj(x2, norm1_g.reshape(1, D_MODEL).astype(F32), w_main, w_gate)
    fm_b = _projt(h, wt_b, F32, "proj_fm_b")
    fm_d = _projt(h, wt_d, BF16, "proj_fm_d")

    vt_a = _projt(h, wt_av, BF16, "proj_fm_av", chunk=A_CHUNK)
    gates_t = _projt(h, wg16.T.astype(BF16), F32, "proj_gates_t", chunk=A_CHUNK)
    y_a = _mlstm(main, vt_a, gates, gates_t, gate_bias, gate_bias_col, conv_w,
                 a_norm_g.reshape(1, A_WIDTH).astype(F32), bsz, seq)

    qt_b, k_b, v_b = _gqa_prep(fm_b, cos, sin, b_qnorm_g.reshape(B_DH, 1).astype(F32),
                               b_knorm_g.reshape(B_DH, 1).astype(F32), seq)
    y_b = _gqa_attn(qt_b, k_b, v_b, bsz, seq)

    y_c = _natten(main, _natten_bias_table(c_rpb, seq // GRID_W), bsz, seq)

    lambda_init = 0.8 - 0.6 * math.exp(-0.3 * l)
    vec = lambda a: a.reshape(1, D_DH).astype(F32)
    y_d = _diff_attn(fm_d, main, vec(d_lq1), vec(d_lk1), vec(d_lq2), vec(d_lk2),
                     d_subln_g.reshape(D_DV, 1).astype(F32), lambda_init, bsz, seq)

    w_up = jnp.stack([w_up_a, w_up_b, w_up_c, w_up_d], axis=0).astype(BF16)
    x2 = _merge(x2, main, y_a, y_b, y_c, y_d, w_up, w_out.astype(BF16))
    return _ffn(x2, norm2_g.reshape(1, D_MODEL).astype(F32), w_ffn_gate.astype(BF16), w_ffn_up.astype(BF16),
                w_ffn_down.astype(BF16), final_g.reshape(1, D_MODEL).astype(F32), final)


def kernel(x, norm1_g, w_in, a_conv_w, a_gate_bias, a_norm_g, b_qnorm_g, b_knorm_g, c_rpb, d_lambda_q1, d_lambda_k1, d_lambda_q2, d_lambda_k2, d_subln_g, w_up_a, w_up_b, w_up_c, w_up_d, w_out, norm2_g, w_ffn_gate, w_ffn_up, w_ffn_down, final_g):
    bsz, seq, _ = x.shape
    assert seq % KC == 0 and seq % GRID_W == 0
    cos, sin = _rope_tables(seq)
    x2 = x.reshape(bsz * seq, D_MODEL)
    depth = norm1_g.shape[0]
    for l in range(depth):
        x2 = _layer(x2, l, bsz, seq, cos, sin, norm1_g[l], w_in[l], a_conv_w[l], a_gate_bias[l], a_norm_g[l],
                    b_qnorm_g[l], b_knorm_g[l], c_rpb[l], d_lambda_q1[l], d_lambda_k1[l], d_lambda_q2[l],
                    d_lambda_k2[l], d_subln_g[l], w_up_a[l], w_up_b[l], w_up_c[l], w_up_d[l], w_out[l],
                    norm2_g[l], w_ffn_gate[l], w_ffn_up[l], w_ffn_down[l], final_g, l == depth - 1)
    return x2.reshape(bsz, seq, D_MODEL)
```

```python
import functools
import math

import jax
import jax.numpy as jnp
import numpy as np
from jax import lax
from jax.experimental import pallas as pl
from jax.experimental.pallas import tpu as pltpu

F32 = jnp.float32
BF16 = jnp.bfloat16

D_MODEL = 1024
DEPTH = 2
GRID_W = 64
EPS = 1e-6
N_BRANCH = 4
A_HEADS, A_DH, A_CHUNK = 4, 128, 128
A_WIDTH = A_HEADS * A_DH
B_HEADS, B_KV, B_DH = 8, 2, 64
B_WIDTH = B_HEADS * B_DH
ROPE_THETA = 10000.0
C_HEADS, C_DH, C_WIN_R, C_WIN_C = 8, 64, 8, 16
C_WIDTH = C_HEADS * C_DH
D_HEADS, D_DH = 4, 64
D_DV = 2 * D_DH
D_WIDTH = D_HEADS * D_DV
D_FF = ((8 * D_MODEL + 3 * 256 - 1) // (3 * 256)) * 256
FF_SPLITS = ((0, 1536), (1536, D_FF))

_SIZES = (A_WIDTH, A_WIDTH, A_WIDTH, A_WIDTH, 4 * A_HEADS, B_WIDTH, B_KV * B_DH, B_KV * B_DH,
          C_WIDTH, C_WIDTH, C_WIDTH, 2 * D_HEADS * D_DH, 2 * D_HEADS * D_DH, D_WIDTH, N_BRANCH * D_MODEL)
_NAMES = ("aq", "ak", "av", "ao", "ag", "bq", "bk", "bv", "cq", "ck", "cv", "dq", "dk", "dv", "gl")
_OFF = {}
_o = 0
for _n, _s in zip(_NAMES, _SIZES):
    _OFF[_n] = (_o, _o + _s)
    _o += _s

_MAIN_ORDER = ("gl", "aq", "ak", "ao", "cq", "ck", "cv", "dk")
_MAIN_OFF = {}
_o = 0
for _n in _MAIN_ORDER:
    _MAIN_OFF[_n] = _o
    _o += _OFF[_n][1] - _OFF[_n][0]
MAIN_N = _o

KC = 512
VMEM_LIMIT = 56 * 1024 * 1024


def _cparams(sem, vmem=VMEM_LIMIT):
    return pltpu.CompilerParams(dimension_semantics=sem, vmem_limit_bytes=vmem)


def _sigmoid(x):
    return 1.0 / (1.0 + jnp.exp(-x))


def _aligned(x, m):
    return x if isinstance(x, int) else pl.multiple_of(x, m)


def _proj_kernel(x_ref, g_ref, w_ref, wg_ref, o_ref, og_ref, h_ref):
    @pl.when(pl.program_id(1) == 0)
    def _():
        x = x_ref[...]
        ms = jnp.mean(x * x, axis=-1, keepdims=True)
        hb = (x * lax.rsqrt(ms + EPS) * g_ref[...]).astype(BF16)
        h_ref[...] = hb
        og_ref[...] = jnp.dot(hb, wg_ref[...], preferred_element_type=F32)

    o_ref[...] = jnp.dot(h_ref[...], w_ref[...], preferred_element_type=F32).astype(o_ref.dtype)


def _proj(x2, g, w_main, w_gate):
    t = x2.shape[0]
    tm = min(2048, t)
    n = w_main.shape[1]
    tn = 768 if n % 768 == 0 else 1024
    ng = w_gate.shape[1]
    return pl.pallas_call(
        _proj_kernel,
        out_shape=(jax.ShapeDtypeStruct((t, n), BF16),
                   jax.ShapeDtypeStruct((t, ng), F32),
                   jax.ShapeDtypeStruct((t, D_MODEL), BF16)),
        grid=(t // tm, n // tn),
        in_specs=[pl.BlockSpec((tm, D_MODEL), lambda i, j: (i, 0)),
                  pl.BlockSpec((1, D_MODEL), lambda i, j: (0, 0)),
                  pl.BlockSpec((D_MODEL, tn), lambda i, j: (0, j)),
                  pl.BlockSpec((D_MODEL, ng), lambda i, j: (0, 0))],
        out_specs=(pl.BlockSpec((tm, tn), lambda i, j: (i, j)),
                   pl.BlockSpec((tm, ng), lambda i, j: (i, 0)),
                   pl.BlockSpec((tm, D_MODEL), lambda i, j: (i, 0))),
        compiler_params=_cparams(("parallel", "arbitrary")),
        name="proj_main",
    )(x2, g, w_main, w_gate)


def _projt_kernel(h_ref, wt_ref, o_ref, *, nchunk, chunk):
    for c in range(nchunk):
        hc = h_ref[c * chunk:(c + 1) * chunk, :]
        o_ref[c] = lax.dot_general(wt_ref[...], hc, (((1,), (1,)), ((), ())),
                                   preferred_element_type=F32).astype(o_ref.dtype)


def _projt(h, wt, out_dtype, name, chunk=KC):
    t = h.shape[0]
    n = wt.shape[0]
    tm = min(2048, t)
    tn = n if n <= 768 else (768 if n % 768 == 0 else 512)
    nchunk = tm // chunk
    return pl.pallas_call(
        functools.partial(_projt_kernel, nchunk=nchunk, chunk=chunk),
        out_shape=jax.ShapeDtypeStruct((t // chunk, n, chunk), out_dtype),
        grid=(t // tm, n // tn),
        in_specs=[pl.BlockSpec((tm, D_MODEL), lambda i, j: (i, 0)),
                  pl.BlockSpec((tn, D_MODEL), lambda i, j: (j, 0))],
        out_specs=pl.BlockSpec((nchunk, tn, chunk), lambda i, j: (i, j, 0)),
        compiler_params=_cparams(("parallel", "arbitrary")),
        name=name,
    )(h, wt)


def _log_sigmoid(x):
    return jnp.minimum(x, 0.0) - jnp.log1p(jnp.exp(-jnp.abs(x)))


def _tri_dot(mat, x):
    hi = x.astype(BF16)
    r1 = x - hi.astype(F32)
    mid = r1.astype(BF16)
    lo = (r1 - mid.astype(F32)).astype(BF16)
    return (jnp.dot(mat, hi, preferred_element_type=F32)
            + jnp.dot(mat, mid, preferred_element_type=F32)
            + jnp.dot(mat, lo, preferred_element_type=F32))


def _split3(x):
    hi = x.astype(BF16)
    r1 = x - hi.astype(F32)
    mid = r1.astype(BF16)
    return hi, mid, (r1 - mid.astype(F32)).astype(BF16)


def _tri_dot_r(x, mat):
    return sum(jnp.dot(piece, mat, preferred_element_type=F32) for piece in _split3(x))


NROWS = 8


def _mlstm_kernel(q_ref, k_ref, vt_ref, o_ref, g_ref, gt_ref, gb_ref, gbc_ref, cw_ref, ng_ref, y_ref,
                  qs_ref, ks_ref, hs_ref, st_ref, m_ref, *, seq):
    L = A_CHUNK
    nc = seq // L
    hw = 2 * A_DH

    rowi = lax.broadcasted_iota(jnp.int32, (L, hw), 0)

    def conv_body(c, carry):
        r0 = pl.multiple_of(c * L, L)
        pstart = pl.multiple_of(jnp.maximum(r0 - 16, 0), 16)
        nstart = pl.multiple_of(jnp.minimum(r0 + L, seq - 16), 16)
        has_prev = jnp.where(c > 0, 1.0, 0.0)
        has_next = jnp.where(c < nc - 1, 1.0, 0.0)
        for src, dst, woff, scale in ((q_ref, qs_ref, 0, 1.0), (k_ref, ks_ref, hw, A_DH ** -0.5)):
            xc = src[pl.ds(r0, L), :].astype(F32)
            prev = src[pl.ds(pstart, 16), :].astype(F32)[15:16] * has_prev
            nxt = src[pl.ds(nstart, 16), :].astype(F32)[0:1] * has_next
            xp = jnp.where(rowi == 0, prev, pltpu.roll(xc, 1, 0))
            xn = jnp.where(rowi == L - 1, nxt, pltpu.roll(xc, L - 1, 0))
            w = cw_ref[:, woff:woff + hw]
            y = xp * w[0:1] + xc * w[1:2] + xn * w[2:3]
            y = y * _sigmoid(y)
            dst[pl.ds(r0, L), :] = (y * scale).astype(BF16)
        return carry

    lax.fori_loop(0, nc, conv_body, 0)

    st_ref[...] = jnp.zeros(st_ref.shape, F32)
    m_ref[...] = jnp.zeros(m_ref.shape, F32)

    ti = lax.broadcasted_iota(jnp.int32, (L, L), 0)
    tj = lax.broadcasted_iota(jnp.int32, (L, L), 1)
    lower = tj <= ti
    upper = tj >= ti
    lmat = jnp.where(lower, 1.0, 0.0).astype(BF16)
    umat = jnp.where(upper, 1.0, 0.0).astype(BF16)
    nt = (((1,), (1,)), ((), ()))

    def step(cf, cb):
        chains = []
        for d, c in ((0, cf), (1, cb)):
            r0 = pl.multiple_of(c * L, L)
            gcol = g_ref[pl.ds(r0, L), :] + gb_ref[...]
            bcol_all = _tri_dot(lmat if d == 0 else umat, _log_sigmoid(gcol))
            grow = gt_ref[c] + gbc_ref[...]
            brow_all = _tri_dot_r(_log_sigmoid(grow), umat if d == 0 else lmat)
            for hh in range(2):
                il = 4 * d + hh
                fl = il + 2
                ch = dict(d=d, idx=2 * d + hh, c=c)
                ch["acol"] = gcol[:, il:il + 1] - bcol_all[:, fl:fl + 1]
                ch["irow"] = grow[il:il + 1, :]
                ch["brow"] = brow_all[fl:fl + 1, :]
                ch["q"] = qs_ref[pl.ds(r0, L), hh * A_DH:(hh + 1) * A_DH]
                ch["k"] = ks_ref[pl.ds(r0, L), hh * A_DH:(hh + 1) * A_DH]
                ch["vt"] = vt_ref[c, hh * A_DH:(hh + 1) * A_DH, :]
                ch["state"] = st_ref[ch["idx"]]
                ch["st"] = lax.dot_general(ch["k"], ch["q"], nt, preferred_element_type=F32)
                ch["it"] = lax.dot_general(ch["state"].astype(BF16), ch["q"], nt,
                                           preferred_element_type=F32)
                chains.append(ch)
        for ch in chains:
            d, brow = ch["d"], ch["brow"]
            mask = upper if d == 0 else lower
            m_prev = m_ref[ch["idx"]][:, 0:1]
            dmt = jnp.where(mask, brow + ch["acol"], -jnp.inf)
            inter = brow + m_prev
            mt = jnp.maximum(inter, jnp.max(dmt, axis=0, keepdims=True))
            ch["w_inter"] = jnp.exp(inter - mt)
            ch["floor"] = jnp.exp(-mt)
            sqk = ch["st"] * jnp.exp(dmt - mt)
            ch["sqk_sum"] = jnp.sum(sqk, axis=0, keepdims=True)
            ch["sqk"] = sqk.astype(BF16)
            bl = brow[:, L - 1:L] if d == 0 else brow[:, 0:1]
            gvec = bl - brow + ch["irow"]
            m_new = jnp.maximum(bl + m_prev, jnp.max(gvec, axis=1, keepdims=True))
            ch["wc"] = jnp.exp(bl + m_prev - m_new)
            ws = jnp.exp(gvec - m_new)
            ch["lhs"] = jnp.concatenate([ch["vt"].astype(F32) * ws, jnp.broadcast_to(ws, (NROWS, L))],
                                        axis=0).astype(BF16)
            m_ref[ch["idx"]] = jnp.broadcast_to(m_new, (1, 128))
        for ch in chains:
            ch["pv"] = jnp.dot(ch["vt"], ch["sqk"], preferred_element_type=F32)
            ch["upd"] = jnp.dot(ch["lhs"], ch["k"], preferred_element_type=F32)
        outs = {0: [], 1: []}
        for ch in chains:
            num = ch["w_inter"] * ch["it"][:A_DH] + ch["pv"]
            den = ch["w_inter"] * ch["it"][A_DH:A_DH + 1] + ch["sqk_sum"]
            outs[ch["d"]].append(num / jnp.maximum(jnp.abs(den), ch["floor"]))
            st_ref[ch["idx"]] = ch["wc"] * ch["state"] + ch["upd"]
        return outs

    def finalize(tots, r0):
        parts = []
        for tot in tots:
            mu = jnp.mean(tot, axis=0, keepdims=True)
            cen = tot - mu
            var = jnp.mean(cen * cen, axis=0, keepdims=True)
            parts.append((cen * lax.rsqrt(var + EPS)).T)
        hn = jnp.concatenate(parts, axis=1) * ng_ref[...]
        y_ref[pl.ds(r0, L), :] = (hn * _sigmoid(o_ref[pl.ds(r0, L), :].astype(F32))).astype(y_ref.dtype)

    def first_half(j, carry):
        outs = step(j, nc - 1 - j)
        for d, c in ((0, j), (1, nc - 1 - j)):
            for hh in range(2):
                hs_ref[c, hh] = outs[d][hh]
        return carry

    def second_half(j, carry):
        outs = step(j, nc - 1 - j)
        for d, c in ((0, j), (1, nc - 1 - j)):
            finalize([hs_ref[c, hh] + outs[d][hh] for hh in range(2)], pl.multiple_of(c * L, L))
        return carry

    lax.fori_loop(0, nc // 2, first_half, 0)
    lax.fori_loop(nc // 2, nc, second_half, 0)


def _mlstm(main, vt, gates, gates_t, gate_bias, gate_bias_col, conv_w, norm_g, bsz, seq):
    t = bsz * seq
    hw = 2 * A_DH
    nc = seq // A_CHUNK
    assert nc % 2 == 0

    def col(name):
        base = _MAIN_OFF[name] // hw
        return pl.BlockSpec((seq, hw), lambda b, p: (b, base + p))

    return pl.pallas_call(
        functools.partial(_mlstm_kernel, seq=seq),
        out_shape=jax.ShapeDtypeStruct((t, A_WIDTH), BF16),
        grid=(bsz, 2),
        in_specs=[col("aq"), col("ak"),
                  pl.BlockSpec((nc, hw, A_CHUNK), lambda b, p: (b, p, 0)),
                  col("ao"),
                  pl.BlockSpec((seq, 128), lambda b, p: (b, p)),
                  pl.BlockSpec((nc, 8, A_CHUNK), lambda b, p: (b, p, 0)),
                  pl.BlockSpec((1, 128), lambda b, p: (0, p)),
                  pl.BlockSpec((8, 1), lambda b, p: (p, 0)),
                  pl.BlockSpec((None, 3, 2 * hw), lambda b, p: (p, 0, 0)),
                  pl.BlockSpec((1, hw), lambda b, p: (0, p))],
        out_specs=pl.BlockSpec((seq, hw), lambda b, p: (b, p)),
        scratch_shapes=[pltpu.VMEM((seq, hw), BF16), pltpu.VMEM((seq, hw), BF16),
                        pltpu.VMEM((nc, 2, A_DH, A_CHUNK), F32),
                        pltpu.VMEM((4, A_DH + NROWS, A_DH), F32),
                        pltpu.VMEM((4, 1, 128), F32)],
        compiler_params=_cparams(("parallel", "arbitrary")),
        name="mlstm",
    )(main, main, vt, main, gates, gates_t, gate_bias, gate_bias_col, conv_w, norm_g)


LOG2E = 1.4426950408889634


def _stash_scores(s, s_ref, mx_ref, idx):
    tk, tq = s.shape
    s_ref[idx] = s
    mx_ref[idx] = jnp.max(s.reshape(tk // 8, 8, tq), axis=0)


ONES_ROWS = 8


def _with_ones(vc):
    return jnp.concatenate([vc, jnp.ones((ONES_ROWS, vc.shape[1]), vc.dtype)], axis=0)


def _flash_update(s_ref, mx_ref, tile, vc1, m_ref, acc_ref, qb, shift=None):
    m_old = m_ref[qb, tile]
    m_new = jnp.maximum(m_old, jnp.max(mx_ref[tile], axis=0, keepdims=True))
    alpha = jnp.exp2(m_old - m_new)
    p = jnp.exp2(s_ref[tile] - (m_new if shift is None else m_new - shift))
    acc_ref[qb, tile] = alpha * acc_ref[qb, tile] + jnp.dot(vc1, p.astype(BF16), preferred_element_type=F32)
    m_ref[qb, tile] = m_new


def _normalised(acc, dv):
    return acc[:dv] / acc[dv:dv + 1]


HK = KC // 2


def _pipelined_sweep(nkc, nq, scores, consume, scores_offdiag=None):
    total = nkc * nq
    scores(0, 0, 0)

    def body(it, carry):
        c, i = it // nq, it % nq
        nxt = it + 1
        c2, i2 = nxt // nq, nxt % nq

        def iteration(score_fn):
            def run():
                score_fn(c, i, 1)
                consume(c, i, 0)
                score_fn(c2, i2, 0)
                consume(c, i, 1)
            return run

        if scores_offdiag is None:
            iteration(scores)()
        else:
            lax.cond((c == i) | (c2 == i2), iteration(scores), iteration(scores_offdiag))
        return carry

    lax.fori_loop(0, total - 1, body, 0)
    scores(nkc - 1, nq - 1, 1)
    consume(nkc - 1, nq - 1, 0)
    consume(nkc - 1, nq - 1, 1)


def _gqa_prep_kernel(fm_ref, cos_ref, sin_ref, qg_ref, kg_ref, q_ref, k_ref, v_ref):
    x = fm_ref[0]
    cos = cos_ref[...]
    sin = sin_ref[...]

    def norm_rope(xh, g):
        ms = jnp.mean(xh * xh, axis=0, keepdims=True)
        xn = xh * lax.rsqrt(ms + EPS) * g
        partner = jnp.concatenate([xn[16:32], xn[0:16], xn[48:64], xn[32:48]], axis=0)
        return xn * cos + partner * sin

    for h in range(B_HEADS):
        qh = norm_rope(x[h * B_DH:(h + 1) * B_DH], qg_ref[...]) * (B_DH ** -0.5 * LOG2E)
        q_ref[0, h * B_DH:(h + 1) * B_DH, :] = qh.astype(BF16)
    ks = [norm_rope(x[B_WIDTH + g * B_DH:B_WIDTH + (g + 1) * B_DH], kg_ref[...]) for g in range(B_KV)]
    k_ref[...] = jnp.concatenate(ks, axis=0).T.astype(BF16)
    v_ref[0] = x[B_WIDTH + B_KV * B_DH:].astype(BF16)


def _gqa_prep(fm_b, cos, sin, qg, kg, seq):
    nchunks = fm_b.shape[0]
    t = nchunks * KC
    spc = seq // KC
    kvw = B_KV * B_DH
    return pl.pallas_call(
        _gqa_prep_kernel,
        out_shape=(jax.ShapeDtypeStruct((nchunks, B_WIDTH, KC), BF16),
                   jax.ShapeDtypeStruct((t, kvw), BF16),
                   jax.ShapeDtypeStruct((nchunks, kvw, KC), BF16)),
        grid=(nchunks,),
        in_specs=[pl.BlockSpec((1, B_WIDTH + 2 * kvw, KC), lambda i: (i, 0, 0)),
                  pl.BlockSpec((B_DH, KC), lambda i: (0, i % spc)),
                  pl.BlockSpec((B_DH, KC), lambda i: (0, i % spc)),
                  pl.BlockSpec((B_DH, 1), lambda i: (0, 0)),
                  pl.BlockSpec((B_DH, 1), lambda i: (0, 0))],
        out_specs=(pl.BlockSpec((1, B_WIDTH, KC), lambda i: (i, 0, 0)),
                   pl.BlockSpec((KC, kvw), lambda i: (i, 0)),
                   pl.BlockSpec((1, kvw, KC), lambda i: (i, 0, 0))),
        compiler_params=_cparams(("parallel",)),
        name="gqa_prep",
    )(fm_b, cos, sin, qg, kg)


def _gqa_attn_kernel(q_ref, k_ref, v_ref, o_ref, qpad_ref, s0_ref, s1_ref, mx0_ref, mx1_ref,
                     m_ref, acc_ref, *, nkc, tq):
    g = pl.program_id(1)
    rep = B_HEADS // B_KV
    nq = q_ref.shape[0]
    row = lax.broadcasted_iota(jnp.int32, (2 * B_DH, tq), 0)
    sel = (row // B_DH) == g

    def pad_queries(i, carry):
        for r in range(rep):
            qh = q_ref[i, r * B_DH:(r + 1) * B_DH, :].astype(F32)
            qpad_ref[i, r] = jnp.where(sel, jnp.concatenate([qh, qh], axis=0), 0.0).astype(BF16)
        return carry

    lax.fori_loop(0, nq, pad_queries, 0)
    m_ref[...] = jnp.full(m_ref.shape, -jnp.inf, F32)
    acc_ref[...] = jnp.zeros(acc_ref.shape, F32)
    slots = ((s0_ref, mx0_ref), (s1_ref, mx1_ref))

    def scores(c, i, half):
        s_ref, mx_ref = slots[half]
        kc = k_ref[pl.ds(_aligned(c * KC + half * HK, HK), HK), :]
        for r in range(rep):
            _stash_scores(jnp.dot(kc, qpad_ref[i, r], preferred_element_type=F32), s_ref, mx_ref, r)

    def consume(c, i, half):
        s_ref, mx_ref = slots[half]
        vc1 = _with_ones(v_ref[c, :, half * HK:(half + 1) * HK])
        for r in range(rep):
            _flash_update(s_ref, mx_ref, r, vc1, m_ref, acc_ref, i)

    _pipelined_sweep(nkc, nq, scores, consume)

    def write_out(i, carry):
        outs = [_normalised(acc_ref[i, r], B_DH) for r in range(rep)]
        o_ref[pl.ds(pl.multiple_of(i * tq, tq), tq), :] = jnp.concatenate(outs, axis=0).T.astype(o_ref.dtype)
        return carry

    lax.fori_loop(0, nq, write_out, 0)


def _gqa_attn(qt, k_tok, v_fm, bsz, seq):
    t = bsz * seq
    tq = KC
    nq = seq // tq
    nkc = seq // KC
    rep = B_HEADS // B_KV
    gw = rep * B_DH
    return pl.pallas_call(
        functools.partial(_gqa_attn_kernel, nkc=nkc, tq=tq),
        out_shape=jax.ShapeDtypeStruct((t, B_WIDTH), BF16),
        grid=(bsz, B_KV),
        in_specs=[pl.BlockSpec((nq, gw, tq), lambda b, g: (b, g, 0)),
                  pl.BlockSpec((seq, B_KV * B_DH), lambda b, g: (b, 0)),
                  pl.BlockSpec((nkc, B_DH, KC), lambda b, g: (b, g, 0))],
        out_specs=pl.BlockSpec((seq, gw), lambda b, g: (b, g)),
        scratch_shapes=[pltpu.VMEM((nq, rep, 2 * B_DH, tq), BF16),
                        pltpu.VMEM((rep, HK, tq), F32), pltpu.VMEM((rep, HK, tq), F32),
                        pltpu.VMEM((rep, 8, tq), F32), pltpu.VMEM((rep, 8, tq), F32),
                        pltpu.VMEM((nq, rep, 1, tq), F32),
                        pltpu.VMEM((nq, rep, B_DH + ONES_ROWS, tq), F32)],
        compiler_params=_cparams(("parallel", "arbitrary")),
        name="gqa_attn",
    )(qt, k_tok, v_fm)


NAT_ROWS = 2


def _natten_kernel(q_ref, k_ref, v_ref, *rest, rows):
    bias_refs, o_ref = rest[:NAT_ROWS], rest[NAT_ROWS]
    nkeys = C_WIN_R * GRID_W
    npair = C_HEADS // 2
    own = (lax.broadcasted_iota(jnp.int32, (2 * GRID_W, 2 * C_DH), 0) // GRID_W
           == lax.broadcasted_iota(jnp.int32, (2 * GRID_W, 2 * C_DH), 1) // C_DH)
    starts = []
    for rr in range(NAT_ROWS):
        r = pl.program_id(1) * NAT_ROWS + rr
        rs = jnp.clip(r - C_WIN_R // 2, 0, rows - C_WIN_R)
        starts.append(pl.multiple_of(rs * GRID_W, GRID_W))
    units = [(rr, p) for rr in range(NAT_ROWS) for p in range(npair)]
    scores = []
    for rr, p in units:
        kp = k_ref[pl.ds(starts[rr], nkeys), p * 128:(p + 1) * 128]
        qp = q_ref[rr * GRID_W:(rr + 1) * GRID_W, p * 128:(p + 1) * 128].astype(F32)
        qm = jnp.where(own, jnp.concatenate([qp, qp], axis=0), 0.0).astype(BF16)
        s = lax.dot_general(qm, kp, (((1,), (1,)), ((), ())), preferred_element_type=F32)
        scores.append(s + bias_refs[rr][0, 2 * p:2 * p + 2].reshape(2 * GRID_W, nkeys))
    probs = []
    for s in scores:
        e = jnp.exp(s - jnp.max(s, axis=1, keepdims=True))
        probs.append((e.astype(BF16), jnp.sum(e, axis=1, keepdims=True)))
    outs = [[] for _ in range(NAT_ROWS)]
    for (rr, p), (e, l) in zip(units, probs):
        vp = v_ref[pl.ds(starts[rr], nkeys), p * 128:(p + 1) * 128]
        o = jnp.where(own, jnp.dot(e, vp, preferred_element_type=F32) / l, 0.0)
        outs[rr].append(o[:GRID_W] + o[GRID_W:])
    o_ref[...] = jnp.concatenate([jnp.concatenate(o, axis=1) for o in outs], axis=0).astype(o_ref.dtype)


def _natten_bias_table(rpb, rows):
    c = np.arange(GRID_W)[:, None]
    kc = np.arange(GRID_W)[None, :]
    cs = np.clip(c - C_WIN_C // 2, 0, GRID_W - C_WIN_C)
    valid = (kc >= cs) & (kc < cs + C_WIN_C)
    nd = 2 * C_WIN_C - 1
    onehot = (kc - c + (C_WIN_C - 1))[None] == np.arange(nd)[:, None, None]
    toe = jnp.sum(jnp.where(onehot[None, None], rpb.astype(F32)[:, :, :, None, None], 0.0), axis=2)
    toe = jnp.where(valid[None, None], toe, -1e30)
    tbl = jnp.stack([toe[:, C_WIN_R - 1 - v:2 * C_WIN_R - 1 - v] for v in range(C_WIN_R)], axis=0)
    return jnp.swapaxes(tbl, 2, 3).reshape(C_WIN_R, C_HEADS, GRID_W, C_WIN_R * GRID_W)


def _natten(main, bias_tbl, bsz, seq):
    t = bsz * seq
    rows = seq // GRID_W
    assert rows >= C_WIN_R
    half = C_WIN_R // 2

    def col(name):
        return _MAIN_OFF[name] // C_WIDTH

    cq, ck, cv = col("cq"), col("ck"), col("cv")
    assert rows % NAT_ROWS == 0
    steps = rows // NAT_ROWS

    def bias_spec(rr):
        def variant(b, i):
            r = i * NAT_ROWS + rr
            return (r - jnp.clip(r - half, 0, rows - C_WIN_R), 0, 0, 0)
        return pl.BlockSpec((1, C_HEADS, GRID_W, C_WIN_R * GRID_W), variant)

    return pl.pallas_call(
        functools.partial(_natten_kernel, rows=rows),
        out_shape=jax.ShapeDtypeStruct((t, C_WIDTH), BF16),
        grid=(bsz, steps),
        in_specs=[pl.BlockSpec((NAT_ROWS * GRID_W, C_WIDTH), lambda b, i: (b * steps + i, cq)),
                  pl.BlockSpec((seq, C_WIDTH), lambda b, i: (b, ck)),
                  pl.BlockSpec((seq, C_WIDTH), lambda b, i: (b, cv))]
                 + [bias_spec(rr) for rr in range(NAT_ROWS)],
        out_specs=pl.BlockSpec((NAT_ROWS * GRID_W, C_WIDTH), lambda b, i: (b * steps + i, 0)),
        compiler_params=_cparams(("parallel", "arbitrary")),
        name="natten",
    )(main, main, main, *([bias_tbl] * NAT_ROWS))


def _diff_attn_kernel(slopes_ref, q1_ref, q2_ref, k1_ref, k2_ref, v_ref, lq1_ref, lk1_ref, lq2_ref, lk2_ref,
                      sg_ref, o_ref, qpad_ref, e_ref, esl_ref, bias_ref, shift_ref, s0_ref, s1_ref, mx0_ref, mx1_ref,
                      m_ref, acc_ref, *, nkc, tq, lambda_init):
    pair = pl.program_id(1)
    nq = q1_ref.shape[0]
    row = lax.broadcasted_iota(jnp.int32, (2 * D_DH, tq), 0)

    def pad_queries(i, carry):
        for c, qr in enumerate((q1_ref, q2_ref)):
            qf = qr[i].astype(F32)
            for hh in range(2):
                qpad_ref[i, 2 * c + hh] = jnp.where((row // D_DH) == hh, qf, 0.0).astype(BF16)
        return carry

    lax.fori_loop(0, nq, pad_queries, 0)
    m_ref[...] = jnp.full(m_ref.shape, -jnp.inf, F32)
    acc_ref[...] = jnp.zeros(acc_ref.shape, F32)
    kk = lax.broadcasted_iota(jnp.int32, (HK, tq), 0)
    qq = lax.broadcasted_iota(jnp.int32, (HK, tq), 1)
    e = (kk - qq).astype(F32)
    e_ref[...] = e
    for hh in range(2):
        slope = slopes_ref[2 * pair + hh]
        esl_ref[0, hh] = e * slope
        esl_ref[1, hh] = e * (-slope)
    slots = ((s0_ref, mx0_ref), (s1_ref, mx1_ref))
    krefs = (k1_ref, k2_ref)

    def qk(c, i, half, comp, hh):
        r0 = _aligned(c * KC + half * HK, HK)
        return jnp.dot(krefs[comp][pl.ds(r0, HK), :], qpad_ref[i, 2 * comp + hh], preferred_element_type=F32)

    def scores(c, i, half):
        s_ref, mx_ref = slots[half]
        off = lax.convert_element_type(c * KC + half * HK - i * tq, F32)
        dist = jnp.abs(e_ref[...] + off)
        for hh in range(2):
            bias_ref[half, hh] = dist * (-slopes_ref[2 * pair + hh])
        for hh in range(2):
            for comp in range(2):
                _stash_scores(qk(c, i, half, comp, hh) + bias_ref[half, hh], s_ref, mx_ref, 2 * comp + hh)
                shift_ref[4 * half + 2 * comp + hh] = 0.0

    def scores_offdiag(c, i, half):
        s_ref, mx_ref = slots[half]
        off = lax.convert_element_type(c * KC + half * HK - i * tq, F32)
        after = (c > i).astype(jnp.int32)
        sign = jnp.where(c > i, -1.0, 1.0)
        for hh in range(2):
            shift = sign * slopes_ref[2 * pair + hh] * off
            for comp in range(2):
                tile = 2 * comp + hh
                s = qk(c, i, half, comp, hh) + esl_ref[after, hh]
                s_ref[tile] = s
                mx_ref[tile] = jnp.max(s.reshape(HK // 8, 8, tq), axis=0) + shift
                shift_ref[4 * half + tile] = shift

    def consume(c, i, half):
        s_ref, mx_ref = slots[half]
        for hh in range(2):
            vc1 = _with_ones(v_ref[c, hh * D_DV:(hh + 1) * D_DV, half * HK:(half + 1) * HK])
            for comp in range(2):
                tile = 2 * comp + hh
                _flash_update(s_ref, mx_ref, tile, vc1, m_ref, acc_ref, i, shift=shift_ref[4 * half + tile])

    _pipelined_sweep(nkc, nq, scores, consume, scores_offdiag)
    lam = (jnp.exp(jnp.sum(lq1_ref[...] * lk1_ref[...], axis=1, keepdims=True))
           - jnp.exp(jnp.sum(lq2_ref[...] * lk2_ref[...], axis=1, keepdims=True)) + lambda_init)

    def write_out(i, carry):
        outs = []
        for hh in range(2):
            o = _normalised(acc_ref[i, hh], D_DV) - lam * _normalised(acc_ref[i, 2 + hh], D_DV)
            ms = jnp.mean(o * o, axis=0, keepdims=True)
            outs.append(o * lax.rsqrt(ms + EPS) * sg_ref[...] * (1.0 - lambda_init))
        o_ref[pl.ds(pl.multiple_of(i * tq, tq), tq), :] = jnp.concatenate(outs, axis=0).T.astype(o_ref.dtype)
        return carry

    lax.fori_loop(0, nq, write_out, 0)


def _diff_attn(fm_d, main, lq1, lk1, lq2, lk2, subln_g, lambda_init, bsz, seq):
    t = bsz * seq
    tq = KC
    nq = seq // tq
    nkc = seq // KC
    pw = 2 * D_DH
    vw = 2 * D_DV
    dk0 = _MAIN_OFF["dk"] // pw
    vrow0 = (2 * D_HEADS * D_DH) // vw
    slopes = jnp.asarray([LOG2E * 2.0 ** (-8.0 * (h + 1) / D_HEADS) for h in range(D_HEADS)], F32)

    def qspec(c):
        return pl.BlockSpec((nq, pw, tq), lambda b, p: (b, 2 * c + p, 0))

    def kspec(c):
        return pl.BlockSpec((seq, pw), lambda b, p: (b, dk0 + 2 * c + p))

    vec = pl.BlockSpec((1, D_DH), lambda b, p: (0, 0))
    return pl.pallas_call(
        functools.partial(_diff_attn_kernel, nkc=nkc, tq=tq, lambda_init=lambda_init),
        out_shape=jax.ShapeDtypeStruct((t, D_WIDTH), BF16),
        grid=(bsz, D_HEADS // 2),
        in_specs=[pl.BlockSpec(memory_space=pltpu.SMEM),
                  qspec(0), qspec(1), kspec(0), kspec(1),
                  pl.BlockSpec((nkc, vw, KC), lambda b, p: (b, vrow0 + p, 0)),
                  vec, vec, vec, vec,
                  pl.BlockSpec((D_DV, 1), lambda b, p: (0, 0))],
        out_specs=pl.BlockSpec((seq, vw), lambda b, p: (b, p)),
        scratch_shapes=[pltpu.VMEM((nq, 4, 2 * D_DH, tq), BF16),
                        pltpu.VMEM((HK, tq), F32),
                        pltpu.VMEM((2, 2, HK, tq), F32),
                        pltpu.VMEM((2, 2, HK, tq), F32),
                        pltpu.SMEM((8,), F32),
                        pltpu.VMEM((4, HK, tq), F32), pltpu.VMEM((4, HK, tq), F32),
                        pltpu.VMEM((4, 8, tq), F32), pltpu.VMEM((4, 8, tq), F32),
                        pltpu.VMEM((nq, 4, 1, tq), F32),
                        pltpu.VMEM((nq, 4, D_DV + ONES_ROWS, tq), F32)],
        compiler_params=_cparams(("parallel", "arbitrary")),
        name="diff_attn",
    )(slopes, fm_d, fm_d, main, main, fm_d, lq1, lk1, lq2, lk2, subln_g)


def _merge_kernel(x_ref, gl_ref, ya_ref, yb_ref, yc_ref, yd_ref, wup_ref, wout_ref, o_ref):
    merged = None
    for g, y_ref in enumerate((ya_ref, yb_ref, yc_ref, yd_ref)):
        u = jnp.dot(y_ref[...], wup_ref[g], preferred_element_type=F32)
        gate = gl_ref[:, g * D_MODEL:(g + 1) * D_MODEL].astype(F32)
        term = _sigmoid(gate) * u
        merged = term if merged is None else merged + term
    o_ref[...] = x_ref[...] + jnp.dot(merged.astype(BF16), wout_ref[...], preferred_element_type=F32)


def _merge(x2, main, ya, yb, yc, yd, w_up, w_out):
    t = x2.shape[0]
    tm = min(512, t)
    ytile = pl.BlockSpec((tm, 512), lambda i: (i, 0))
    return pl.pallas_call(
        _merge_kernel,
        out_shape=jax.ShapeDtypeStruct((t, D_MODEL), F32),
        grid=(t // tm,),
        in_specs=[pl.BlockSpec((tm, D_MODEL), lambda i: (i, 0)),
                  pl.BlockSpec((tm, N_BRANCH * D_MODEL), lambda i: (i, 0)),
                  ytile, ytile, ytile, ytile,
                  pl.BlockSpec((N_BRANCH, 512, D_MODEL), lambda i: (0, 0, 0)),
                  pl.BlockSpec((D_MODEL, D_MODEL), lambda i: (0, 0))],
        out_specs=pl.BlockSpec((tm, D_MODEL), lambda i: (i, 0)),
        compiler_params=_cparams(("parallel",)),
        name="merge",
    )(x2, main, ya, yb, yc, yd, w_up, w_out)


def _ffn_kernel(x_ref, g_ref, wg_ref, wu_ref, wd_ref, fg_ref, o_ref, *, final):
    x = x_ref[...]
    ms = jnp.mean(x * x, axis=-1, keepdims=True)
    h = (x * lax.rsqrt(ms + EPS) * g_ref[...]).astype(BF16)
    acc = x
    for a, b in FF_SPLITS:
        gt = jnp.dot(h, wg_ref[:, a:b], preferred_element_type=F32)
        up = jnp.dot(h, wu_ref[:, a:b], preferred_element_type=F32)
        act = (gt * _sigmoid(gt)) * up
        acc = acc + jnp.dot(act.astype(BF16), wd_ref[a:b, :], preferred_element_type=F32)
    if final:
        ms2 = jnp.mean(acc * acc, axis=-1, keepdims=True)
        acc = acc * lax.rsqrt(ms2 + EPS) * fg_ref[...]
    o_ref[...] = acc


def _ffn(x2, g, wg, wu, wd, fg, final):
    t = x2.shape[0]
    tm = min(512, t)
    return pl.pallas_call(
        functools.partial(_ffn_kernel, final=final),
        out_shape=jax.ShapeDtypeStruct((t, D_MODEL), F32),
        grid=(t // tm,),
        in_specs=[pl.BlockSpec((tm, D_MODEL), lambda i: (i, 0)),
                  pl.BlockSpec((1, D_MODEL), lambda i: (0, 0)),
                  pl.BlockSpec((D_MODEL, D_FF), lambda i: (0, 0)),
                  pl.BlockSpec((D_MODEL, D_FF), lambda i: (0, 0)),
                  pl.BlockSpec((D_FF, D_MODEL), lambda i: (0, 0)),
                  pl.BlockSpec((1, D_MODEL), lambda i: (0, 0))],
        out_specs=pl.BlockSpec((tm, D_MODEL), lambda i: (i, 0)),
        compiler_params=_cparams(("parallel",)),
        name="ffn",
    )(x2, g, wg, wu, wd, fg)


def _rope_tables(seq):
    tpos = jnp.arange(seq)
    row = (tpos // GRID_W).astype(F32)
    colp = (tpos % GRID_W).astype(F32)
    n_freq = B_DH // 4
    inv = ROPE_THETA ** (-jnp.arange(n_freq, dtype=F32) / n_freq)
    ar = (row[:, None] * inv).T
    ac = (colp[:, None] * inv).T
    cos = jnp.concatenate([jnp.cos(ar), jnp.cos(ar), jnp.cos(ac), jnp.cos(ac)], axis=0)
    sin = jnp.concatenate([-jnp.sin(ar), jnp.sin(ar), -jnp.sin(ac), jnp.sin(ac)], axis=0)
    return cos.astype(F32), sin.astype(F32)


def _w_cols(w_in, name, scale=None):
    a, b = _OFF[name]
    w = w_in[:, a:b]
    return w if scale is None else w * scale


def _layer(x2, l, bsz, seq, cos, sin, norm1_g, w_in, a_conv_w, a_gate_bias, a_norm_g, b_qnorm_g, b_knorm_g,
           c_rpb, d_lq1, d_lk1, d_lq2, d_lk2, d_subln_g, w_up_a, w_up_b, w_up_c, w_up_d, w_out,
           norm2_g, w_ffn_gate, w_ffn_up, w_ffn_down, final_g, final):
    scales = {"cq": C_DH ** -0.5, "dq": D_DH ** -0.5 * LOG2E}
    w_main = jnp.concatenate([_w_cols(w_in, n, scales.get(n)) for n in _MAIN_ORDER], axis=1).astype(BF16)
    ga, _ = _OFF["ag"]
    gate_cols = [ga + ty * A_HEADS + 2 * p + hh for p in range(2) for ty in range(4) for hh in range(2)]
    wg16 = w_in[:, jnp.asarray(gate_cols)]
    w_gate = jnp.pad(wg16.reshape(D_MODEL, 2, 8), ((0, 0), (0, 0), (0, 120))).reshape(D_MODEL, 256).astype(BF16)
    gb = a_gate_bias[jnp.asarray([c - ga for c in gate_cols])].reshape(2, 8)
    gate_bias = jnp.pad(gb, ((0, 0), (0, 120))).reshape(1, 256).astype(F32)
    gate_bias_col = gb.reshape(16, 1).astype(F32)
    wt_av = _w_cols(w_in, "av").T.astype(BF16)
    wt_b = jnp.concatenate([_w_cols(w_in, n) for n in ("bq", "bk", "bv")], axis=1).T.astype(BF16)
    wt_d = jnp.concatenate([_w_cols(w_in, "dq", scales["dq"]), _w_cols(w_in, "dv")], axis=1).T.astype(BF16)
    cw = a_conv_w.astype(F32)
    conv_w = jnp.stack([jnp.concatenate([cw[:, p * 256:(p + 1) * 256],
                                         cw[:, A_WIDTH + p * 256:A_WIDTH + (p + 1) * 256]], axis=1)
                        for p in range(2)], axis=0)

    main, gates, h = _proj(x2, norm1_g.reshape(1, D_MODEL).astype(F32), w_main, w_gate)
    fm_b = _projt(h, wt_b, F32, "proj_fm_b")
    fm_d = _projt(h, wt_d, BF16, "proj_fm_d")

    vt_a = _projt(h, wt_av, BF16, "proj_fm_av", chunk=A_CHUNK)
    gates_t = _projt(h, wg16.T.astype(BF16), F32, "proj_gates_t", chunk=A_CHUNK)
    y_a = _mlstm(main, vt_a, gates, gates_t, gate_bias, gate_bias_col, conv_w,
                 a_norm_g.reshape(1, A_WIDTH).astype(F32), bsz, seq)

    qt_b, k_b, v_b = _gqa_prep(fm_b, cos, sin, b_qnorm_g.reshape(B_DH, 1).astype(F32),
                               b_knorm_g.reshape(B_DH, 1).astype(F32), seq)
    y_b = _gqa_attn(qt_b, k_b, v_b, bsz, seq)

    y_c = _natten(main, _natten_bias_table(c_rpb, seq // GRID_W), bsz, seq)

    lambda_init = 0.8 - 0.6 * math.exp(-0.3 * l)
    vec = lambda a: a.reshape(1, D_DH).astype(F32)
    y_d = _diff_attn(fm_d, main, vec(d_lq1), vec(d_lk1), vec(d_lq2), vec(d_lk2),
                     d_subln_g.reshape(D_DV, 1).astype(F32), lambda_init, bsz, seq)

    w_up = jnp.stack([w_up_a, w_up_b, w_up_c, w_up_d], axis=0).astype(BF16)
    x2 = _merge(x2, main, y_a, y_b, y_c, y_d, w_up, w_out.astype(BF16))
    return _ffn(x2, norm2_g.reshape(1, D_MODEL).astype(F32), w_ffn_gate.astype(BF16), w_ffn_up.astype(BF16),
                w_ffn_down.astype(BF16), final_g.reshape(1, D_MODEL).astype(F32), final)


def kernel(x, norm1_g, w_in, a_conv_w, a_gate_bias, a_norm_g, b_qnorm_g, b_knorm_g, c_rpb, d_lambda_q1, d_lambda_k1, d_lambda_q2, d_lambda_k2, d_subln_g, w_up_a, w_up_b, w_up_c, w_up_d, w_out, norm2_g, w_ffn_gate, w_ffn_up, w_ffn_down, final_g):
    bsz, seq, _ = x.shape
    assert seq % KC == 0 and seq % GRID_W == 0
    cos, sin = _rope_tables(seq)
    x2 = x.reshape(bsz * seq, D_MODEL)
    depth = norm1_g.shape[0]
    for l in range(depth):
        x2 = _layer(x2, l, bsz, seq, cos, sin, norm1_g[l], w_in[l], a_conv_w[l], a_gate_bias[l], a_norm_g[l],
                    b_qnorm_g[l], b_knorm_g[l], c_rpb[l], d_lambda_q1[l], d_lambda_k1[l], d_lambda_q2[l],
                    d_lambda_k2[l], d_subln_g[l], w_up_a[l], w_up_b[l], w_up_c[l], w_up_d[l], w_out[l],
                    norm2_g[l], w_ffn_gate[l], w_ffn_up[l], w_ffn_down[l], final_g, l == depth - 1)
    return x2.reshape(bsz, seq, D_MODEL)
```

```python
import functools
import math

import jax
import jax.numpy as jnp
import numpy as np
from jax import lax
from jax.experimental import pallas as pl
from jax.experimental.pallas import tpu as pltpu

F32 = jnp.float32
BF16 = jnp.bfloat16

D_MODEL = 1024
DEPTH = 2
GRID_W = 64
EPS = 1e-6
N_BRANCH = 4
A_HEADS, A_DH, A_CHUNK = 4, 128, 128
A_WIDTH = A_HEADS * A_DH
B_HEADS, B_KV, B_DH = 8, 2, 64
B_WIDTH = B_HEADS * B_DH
ROPE_THETA = 10000.0
C_HEADS, C_DH, C_WIN_R, C_WIN_C = 8, 64, 8, 16
C_WIDTH = C_HEADS * C_DH
D_HEADS, D_DH = 4, 64
D_DV = 2 * D_DH
D_WIDTH = D_HEADS * D_DV
D_FF = ((8 * D_MODEL + 3 * 256 - 1) // (3 * 256)) * 256
FF_SPLITS = ((0, 1536), (1536, D_FF))

_SIZES = (A_WIDTH, A_WIDTH, A_WIDTH, A_WIDTH, 4 * A_HEADS, B_WIDTH, B_KV * B_DH, B_KV * B_DH,
          C_WIDTH, C_WIDTH, C_WIDTH, 2 * D_HEADS * D_DH, 2 * D_HEADS * D_DH, D_WIDTH, N_BRANCH * D_MODEL)
_NAMES = ("aq", "ak", "av", "ao", "ag", "bq", "bk", "bv", "cq", "ck", "cv", "dq", "dk", "dv", "gl")
_OFF = {}
_o = 0
for _n, _s in zip(_NAMES, _SIZES):
    _OFF[_n] = (_o, _o + _s)
    _o += _s

_MAIN_ORDER = ("gl", "aq", "ak", "ao", "cq", "ck", "cv", "dk")
_MAIN_OFF = {}
_o = 0
for _n in _MAIN_ORDER:
    _MAIN_OFF[_n] = _o
    _o += _OFF[_n][1] - _OFF[_n][0]
MAIN_N = _o

KC = 512
VMEM_LIMIT = 56 * 1024 * 1024


def _cparams(sem, vmem=VMEM_LIMIT):
    return pltpu.CompilerParams(dimension_semantics=sem, vmem_limit_bytes=vmem)


def _sigmoid(x):
    return 1.0 / (1.0 + jnp.exp(-x))


def _aligned(x, m):
    return x if isinstance(x, int) else pl.multiple_of(x, m)


def _proj_kernel(x_ref, g_ref, w_ref, wg_ref, o_ref, og_ref, h_ref):
    @pl.when(pl.program_id(1) == 0)
    def _():
        x = x_ref[...]
        ms = jnp.mean(x * x, axis=-1, keepdims=True)
        hb = (x * lax.rsqrt(ms + EPS) * g_ref[...]).astype(BF16)
        h_ref[...] = hb
        og_ref[...] = jnp.dot(hb, wg_ref[...], preferred_element_type=F32)

    o_ref[...] = jnp.dot(h_ref[...], w_ref[...], preferred_element_type=F32).astype(o_ref.dtype)


def _proj(x2, g, w_main, w_gate):
    t = x2.shape[0]
    tm = min(2048, t)
    n = w_main.shape[1]
    tn = 768 if n % 768 == 0 else 1024
    ng = w_gate.shape[1]
    return pl.pallas_call(
        _proj_kernel,
        out_shape=(jax.ShapeDtypeStruct((t, n), BF16),
                   jax.ShapeDtypeStruct((t, ng), F32),
                   jax.ShapeDtypeStruct((t, D_MODEL), BF16)),
        grid=(t // tm, n // tn),
        in_specs=[pl.BlockSpec((tm, D_MODEL), lambda i, j: (i, 0)),
                  pl.BlockSpec((1, D_MODEL), lambda i, j: (0, 0)),
                  pl.BlockSpec((D_MODEL, tn), lambda i, j: (0, j)),
                  pl.BlockSpec((D_MODEL, ng), lambda i, j: (0, 0))],
        out_specs=(pl.BlockSpec((tm, tn), lambda i, j: (i, j)),
                   pl.BlockSpec((tm, ng), lambda i, j: (i, 0)),
                   pl.BlockSpec((tm, D_MODEL), lambda i, j: (i, 0))),
        compiler_params=_cparams(("parallel", "arbitrary")),
        name="proj_main",
    )(x2, g, w_main, w_gate)


def _projt_kernel(h_ref, wt_ref, o_ref, *, nchunk, chunk):
    for c in range(nchunk):
        hc = h_ref[c * chunk:(c + 1) * chunk, :]
        o_ref[c] = lax.dot_general(wt_ref[...], hc, (((1,), (1,)), ((), ())),
                                   preferred_element_type=F32).astype(o_ref.dtype)


def _projt(h, wt, out_dtype, name, chunk=KC):
    t = h.shape[0]
    n = wt.shape[0]
    tm = min(2048, t)
    tn = n if n <= 768 else (768 if n % 768 == 0 else 512)
    nchunk = tm // chunk
    return pl.pallas_call(
        functools.partial(_projt_kernel, nchunk=nchunk, chunk=chunk),
        out_shape=jax.ShapeDtypeStruct((t // chunk, n, chunk), out_dtype),
        grid=(t // tm, n // tn),
        in_specs=[pl.BlockSpec((tm, D_MODEL), lambda i, j: (i, 0)),
                  pl.BlockSpec((tn, D_MODEL), lambda i, j: (j, 0))],
        out_specs=pl.BlockSpec((nchunk, tn, chunk), lambda i, j: (i, j, 0)),
        compiler_params=_cparams(("parallel", "arbitrary")),
        name=name,
    )(h, wt)


def _log_sigmoid(x):
    return jnp.minimum(x, 0.0) - jnp.log1p(jnp.exp(-jnp.abs(x)))


def _tri_dot(mat, x):
    hi = x.astype(BF16)
    r1 = x - hi.astype(F32)
    mid = r1.astype(BF16)
    lo = (r1 - mid.astype(F32)).astype(BF16)
    return (jnp.dot(mat, hi, preferred_element_type=F32)
            + jnp.dot(mat, mid, preferred_element_type=F32)
            + jnp.dot(mat, lo, preferred_element_type=F32))


def _split3(x):
    hi = x.astype(BF16)
    r1 = x - hi.astype(F32)
    mid = r1.astype(BF16)
    return hi, mid, (r1 - mid.astype(F32)).astype(BF16)


def _tri_dot_r(x, mat):
    return sum(jnp.dot(piece, mat, preferred_element_type=F32) for piece in _split3(x))


NROWS = 8
A_HPS = 4


def _mlstm_kernel(q_ref, k_ref, vt_ref, o_ref, g_ref, gt_ref, gb_ref, gbc_ref, cw_ref, ng_ref, y_ref,
                  qs_ref, ks_ref, hs_ref, st_ref, m_ref, *, seq):
    L = A_CHUNK
    nc = seq // L
    hw = A_HPS * A_DH

    rowi = lax.broadcasted_iota(jnp.int32, (L, hw), 0)

    def conv_body(c, carry):
        r0 = pl.multiple_of(c * L, L)
        pstart = pl.multiple_of(jnp.maximum(r0 - 16, 0), 16)
        nstart = pl.multiple_of(jnp.minimum(r0 + L, seq - 16), 16)
        has_prev = jnp.where(c > 0, 1.0, 0.0)
        has_next = jnp.where(c < nc - 1, 1.0, 0.0)
        for src, dst, woff, scale in ((q_ref, qs_ref, 0, 1.0), (k_ref, ks_ref, hw, A_DH ** -0.5)):
            xc = src[pl.ds(r0, L), :].astype(F32)
            prev = src[pl.ds(pstart, 16), :].astype(F32)[15:16] * has_prev
            nxt = src[pl.ds(nstart, 16), :].astype(F32)[0:1] * has_next
            xp = jnp.where(rowi == 0, prev, pltpu.roll(xc, 1, 0))
            xn = jnp.where(rowi == L - 1, nxt, pltpu.roll(xc, L - 1, 0))
            w = cw_ref[:, woff:woff + hw]
            y = xp * w[0:1] + xc * w[1:2] + xn * w[2:3]
            y = y * _sigmoid(y)
            dst[pl.ds(r0, L), :] = (y * scale).astype(BF16)
        return carry

    lax.fori_loop(0, nc, conv_body, 0)

    st_ref[...] = jnp.zeros(st_ref.shape, F32)
    m_ref[...] = jnp.zeros(m_ref.shape, F32)

    ti = lax.broadcasted_iota(jnp.int32, (L, L), 0)
    tj = lax.broadcasted_iota(jnp.int32, (L, L), 1)
    lower = tj <= ti
    upper = tj >= ti
    lmat = jnp.where(lower, 1.0, 0.0).astype(BF16)
    umat = jnp.where(upper, 1.0, 0.0).astype(BF16)
    nt = (((1,), (1,)), ((), ()))

    def step(cf, cb):
        chains = []
        for d, c in ((0, cf), (1, cb)):
            r0 = pl.multiple_of(c * L, L)
            gcol = g_ref[pl.ds(r0, L), :] + gb_ref[...]
            bcol_all = _tri_dot(lmat if d == 0 else umat, _log_sigmoid(gcol))
            grow = gt_ref[c] + gbc_ref[...]
            brow_all = _tri_dot_r(_log_sigmoid(grow), umat if d == 0 else lmat)
            for hh in range(A_HPS):
                il = 2 * A_HPS * d + hh
                fl = il + A_HPS
                ch = dict(d=d, idx=A_HPS * d + hh, c=c)
                ch["acol"] = gcol[:, il:il + 1] - bcol_all[:, fl:fl + 1]
                ch["irow"] = grow[il:il + 1, :]
                ch["brow"] = brow_all[fl:fl + 1, :]
                ch["q"] = qs_ref[pl.ds(r0, L), hh * A_DH:(hh + 1) * A_DH]
                ch["k"] = ks_ref[pl.ds(r0, L), hh * A_DH:(hh + 1) * A_DH]
                ch["vt"] = vt_ref[c, hh * A_DH:(hh + 1) * A_DH, :]
                ch["state"] = st_ref[ch["idx"]]
                ch["st"] = lax.dot_general(ch["k"], ch["q"], nt, preferred_element_type=F32)
                ch["it"] = lax.dot_general(ch["state"].astype(BF16), ch["q"], nt,
                                           preferred_element_type=F32)
                chains.append(ch)
        for ch in chains:
            d, brow = ch["d"], ch["brow"]
            mask = upper if d == 0 else lower
            m_prev = m_ref[ch["idx"]][:, 0:1]
            dmt = jnp.where(mask, brow + ch["acol"], -jnp.inf)
            inter = brow + m_prev
            mt = jnp.maximum(inter, jnp.max(dmt, axis=0, keepdims=True))
            ch["w_inter"] = jnp.exp(inter - mt)
            ch["floor"] = jnp.exp(-mt)
            sqk = ch["st"] * jnp.exp(dmt - mt)
            ch["sqk_sum"] = jnp.sum(sqk, axis=0, keepdims=True)
            ch["sqk"] = sqk.astype(BF16)
            bl = brow[:, L - 1:L] if d == 0 else brow[:, 0:1]
            gvec = bl - brow + ch["irow"]
            m_new = jnp.maximum(bl + m_prev, jnp.max(gvec, axis=1, keepdims=True))
            ch["wc"] = jnp.exp(bl + m_prev - m_new)
            ws = jnp.exp(gvec - m_new)
            ch["lhs"] = jnp.concatenate([ch["vt"].astype(F32) * ws, jnp.broadcast_to(ws, (NROWS, L))],
                                        axis=0).astype(BF16)
            m_ref[ch["idx"]] = jnp.broadcast_to(m_new, (1, 128))
        for ch in chains:
            ch["pv"] = jnp.dot(ch["vt"], ch["sqk"], preferred_element_type=F32)
            ch["upd"] = jnp.dot(ch["lhs"], ch["k"], preferred_element_type=F32)
        outs = {0: [], 1: []}
        for ch in chains:
            num = ch["w_inter"] * ch["it"][:A_DH] + ch["pv"]
            den = ch["w_inter"] * ch["it"][A_DH:A_DH + 1] + ch["sqk_sum"]
            outs[ch["d"]].append(num / jnp.maximum(jnp.abs(den), ch["floor"]))
            st_ref[ch["idx"]] = ch["wc"] * ch["state"] + ch["upd"]
        return outs

    def finalize(tots, r0):
        parts = []
        for tot in tots:
            mu = jnp.mean(tot, axis=0, keepdims=True)
            cen = tot - mu
            var = jnp.mean(cen * cen, axis=0, keepdims=True)
            parts.append((cen * lax.rsqrt(var + EPS)).T)
        hn = jnp.concatenate(parts, axis=1) * ng_ref[...]
        y_ref[pl.ds(r0, L), :] = (hn * _sigmoid(o_ref[pl.ds(r0, L), :].astype(F32))).astype(y_ref.dtype)

    def first_half(j, carry):
        outs = step(j, nc - 1 - j)
        for d, c in ((0, j), (1, nc - 1 - j)):
            for hh in range(A_HPS):
                hs_ref[c, hh] = outs[d][hh]
        return carry

    def second_half(j, carry):
        outs = step(j, nc - 1 - j)
        for d, c in ((0, j), (1, nc - 1 - j)):
            finalize([hs_ref[c, hh] + outs[d][hh] for hh in range(A_HPS)], pl.multiple_of(c * L, L))
        return carry

    lax.fori_loop(0, nc // 2, first_half, 0)
    lax.fori_loop(nc // 2, nc, second_half, 0)


def _mlstm(main, vt, gates, gates_t, gate_bias, gate_bias_col, conv_w, norm_g, bsz, seq):
    t = bsz * seq
    hw = A_HPS * A_DH
    nc = seq // A_CHUNK
    ngates = 4 * A_HPS
    assert nc % 2 == 0
    once = pl.Buffered(1)

    def col(name):
        base = _MAIN_OFF[name] // hw
        return pl.BlockSpec((seq, hw), lambda b, p: (b, base + p), pipeline_mode=once)

    return pl.pallas_call(
        functools.partial(_mlstm_kernel, seq=seq),
        out_shape=jax.ShapeDtypeStruct((t, A_WIDTH), BF16),
        grid=(bsz, A_HEADS // A_HPS),
        in_specs=[col("aq"), col("ak"),
                  pl.BlockSpec((nc, hw, A_CHUNK), lambda b, p: (b, p, 0), pipeline_mode=once),
                  col("ao"),
                  pl.BlockSpec((seq, 128), lambda b, p: (b, p), pipeline_mode=once),
                  pl.BlockSpec((nc, ngates, A_CHUNK), lambda b, p: (b, p, 0), pipeline_mode=once),
                  pl.BlockSpec((1, 128), lambda b, p: (0, p)),
                  pl.BlockSpec((ngates, 1), lambda b, p: (p, 0)),
                  pl.BlockSpec((None, 3, 2 * hw), lambda b, p: (p, 0, 0)),
                  pl.BlockSpec((1, hw), lambda b, p: (0, p))],
        out_specs=pl.BlockSpec((seq, hw), lambda b, p: (b, p)),
        scratch_shapes=[pltpu.VMEM((seq, hw), BF16), pltpu.VMEM((seq, hw), BF16),
                        pltpu.VMEM((nc, A_HPS, A_DH, A_CHUNK), F32),
                        pltpu.VMEM((2 * A_HPS, A_DH + NROWS, A_DH), F32),
                        pltpu.VMEM((2 * A_HPS, 1, 128), F32)],
        compiler_params=_cparams(("parallel", "arbitrary")),
        name="mlstm",
    )(main, main, vt, main, gates, gates_t, gate_bias, gate_bias_col, conv_w, norm_g)


LOG2E = 1.4426950408889634


def _stash_scores(s, s_ref, mx_ref, idx):
    tk, tq = s.shape
    s_ref[idx] = s
    mx_ref[idx] = jnp.max(s.reshape(tk // 8, 8, tq), axis=0)


ONES_ROWS = 8


def _with_ones(vc):
    return jnp.concatenate([vc, jnp.ones((ONES_ROWS, vc.shape[1]), vc.dtype)], axis=0)


def _flash_update(s_ref, mx_ref, tile, vc1, m_ref, acc_ref, qb, shift=None):
    m_old = m_ref[qb, tile]
    m_new = jnp.maximum(m_old, jnp.max(mx_ref[tile], axis=0, keepdims=True))
    alpha = jnp.exp2(m_old - m_new)
    p = jnp.exp2(s_ref[tile] - (m_new if shift is None else m_new - shift))
    acc_ref[qb, tile] = alpha * acc_ref[qb, tile] + jnp.dot(vc1, p.astype(BF16), preferred_element_type=F32)
    m_ref[qb, tile] = m_new


def _normalised(acc, dv):
    return acc[:dv] / acc[dv:dv + 1]


HK = KC // 2


def _pipelined_sweep(nkc, nq, scores, consume, scores_offdiag=None, on_diagonal=None, unroll=2):
    total = nkc * nq

    def pair_of(it):
        return it // nq, it % nq

    def step(score_fn, it):
        c, i = pair_of(it)
        c2, i2 = pair_of(it + 1)
        score_fn(c, i, 1)
        consume(c, i, 0)
        score_fn(c2, i2, 0)
        consume(c, i, 1)

    def touches_diagonal(it):
        c, i = pair_of(it)
        c2, i2 = pair_of(it + 1)
        return on_diagonal(c, i, 1) | on_diagonal(c2, i2, 0)

    scores(0, 0, 0)
    trips = (total - 1) // unroll

    def body(j, carry):
        its = [j * unroll + u for u in range(unroll)]

        def trip(score_fn):
            def run():
                for it in its:
                    step(score_fn, it)
            return run

        if scores_offdiag is None:
            trip(scores)()
        else:
            slow = touches_diagonal(its[0])
            for it in its[1:]:
                slow = slow | touches_diagonal(it)
            lax.cond(slow, trip(scores), trip(scores_offdiag))
        return carry

    lax.fori_loop(0, trips, body, 0)
    for it in range(trips * unroll, total - 1):
        step(scores, it)
    scores(nkc - 1, nq - 1, 1)
    consume(nkc - 1, nq - 1, 0)
    consume(nkc - 1, nq - 1, 1)


def _gqa_prep_kernel(fm_ref, cos_ref, sin_ref, qg_ref, kg_ref, q_ref, k_ref, v_ref):
    x = fm_ref[0]
    cos = cos_ref[...]
    sin = sin_ref[...]

    def norm_rope(xh, g):
        ms = jnp.mean(xh * xh, axis=0, keepdims=True)
        xn = xh * lax.rsqrt(ms + EPS) * g
        partner = jnp.concatenate([xn[16:32], xn[0:16], xn[48:64], xn[32:48]], axis=0)
        return xn * cos + partner * sin

    for h in range(B_HEADS):
        qh = norm_rope(x[h * B_DH:(h + 1) * B_DH], qg_ref[...]) * (B_DH ** -0.5 * LOG2E)
        q_ref[0, h * B_DH:(h + 1) * B_DH, :] = qh.astype(BF16)
    ks = [norm_rope(x[B_WIDTH + g * B_DH:B_WIDTH + (g + 1) * B_DH], kg_ref[...]) for g in range(B_KV)]
    k_ref[...] = jnp.concatenate(ks, axis=0).T.astype(BF16)
    v_ref[0] = x[B_WIDTH + B_KV * B_DH:].astype(BF16)


def _gqa_prep(fm_b, cos, sin, qg, kg, seq):
    nchunks = fm_b.shape[0]
    t = nchunks * KC
    spc = seq // KC
    kvw = B_KV * B_DH
    return pl.pallas_call(
        _gqa_prep_kernel,
        out_shape=(jax.ShapeDtypeStruct((nchunks, B_WIDTH, KC), BF16),
                   jax.ShapeDtypeStruct((t, kvw), BF16),
                   jax.ShapeDtypeStruct((nchunks, kvw, KC), BF16)),
        grid=(nchunks,),
        in_specs=[pl.BlockSpec((1, B_WIDTH + 2 * kvw, KC), lambda i: (i, 0, 0)),
                  pl.BlockSpec((B_DH, KC), lambda i: (0, i % spc)),
                  pl.BlockSpec((B_DH, KC), lambda i: (0, i % spc)),
                  pl.BlockSpec((B_DH, 1), lambda i: (0, 0)),
                  pl.BlockSpec((B_DH, 1), lambda i: (0, 0))],
        out_specs=(pl.BlockSpec((1, B_WIDTH, KC), lambda i: (i, 0, 0)),
                   pl.BlockSpec((KC, kvw), lambda i: (i, 0)),
                   pl.BlockSpec((1, kvw, KC), lambda i: (i, 0, 0))),
        compiler_params=_cparams(("parallel",)),
        name="gqa_prep",
    )(fm_b, cos, sin, qg, kg)


def _gqa_attn_kernel(q_ref, k_ref, v_ref, o_ref, qpad_ref, s0_ref, s1_ref, mx0_ref, mx1_ref,
                     m_ref, acc_ref, *, nkc, tq):
    g = pl.program_id(1)
    rep = B_HEADS // B_KV
    nq = q_ref.shape[0]
    row = lax.broadcasted_iota(jnp.int32, (2 * B_DH, tq), 0)
    sel = (row // B_DH) == g

    def pad_queries(i, carry):
        for r in range(rep):
            qh = q_ref[i, r * B_DH:(r + 1) * B_DH, :].astype(F32)
            qpad_ref[i, r] = jnp.where(sel, jnp.concatenate([qh, qh], axis=0), 0.0).astype(BF16)
        return carry

    lax.fori_loop(0, nq, pad_queries, 0)
    m_ref[...] = jnp.full(m_ref.shape, -jnp.inf, F32)
    acc_ref[...] = jnp.zeros(acc_ref.shape, F32)
    slots = ((s0_ref, mx0_ref), (s1_ref, mx1_ref))

    def scores(c, i, half):
        s_ref, mx_ref = slots[half]
        kc = k_ref[pl.ds(_aligned(c * KC + half * HK, HK), HK), :]
        for r in range(rep):
            _stash_scores(jnp.dot(kc, qpad_ref[i, r], preferred_element_type=F32), s_ref, mx_ref, r)

    def consume(c, i, half):
        s_ref, mx_ref = slots[half]
        vc1 = _with_ones(v_ref[c, :, half * HK:(half + 1) * HK])
        for r in range(rep):
            _flash_update(s_ref, mx_ref, r, vc1, m_ref, acc_ref, i)

    _pipelined_sweep(nkc, nq, scores, consume, unroll=4)

    def write_out(i, carry):
        outs = [_normalised(acc_ref[i, r], B_DH) for r in range(rep)]
        o_ref[pl.ds(pl.multiple_of(i * tq, tq), tq), :] = jnp.concatenate(outs, axis=0).T.astype(o_ref.dtype)
        return carry

    lax.fori_loop(0, nq, write_out, 0)


def _gqa_attn(qt, k_tok, v_fm, bsz, seq):
    t = bsz * seq
    tq = KC
    nq = seq // tq
    nkc = seq // KC
    rep = B_HEADS // B_KV
    gw = rep * B_DH
    return pl.pallas_call(
        functools.partial(_gqa_attn_kernel, nkc=nkc, tq=tq),
        out_shape=jax.ShapeDtypeStruct((t, B_WIDTH), BF16),
        grid=(bsz, B_KV),
        in_specs=[pl.BlockSpec((nq, gw, tq), lambda b, g: (b, g, 0)),
                  pl.BlockSpec((seq, B_KV * B_DH), lambda b, g: (b, 0)),
                  pl.BlockSpec((nkc, B_DH, KC), lambda b, g: (b, g, 0))],
        out_specs=pl.BlockSpec((seq, gw), lambda b, g: (b, g)),
        scratch_shapes=[pltpu.VMEM((nq, rep, 2 * B_DH, tq), BF16),
                        pltpu.VMEM((rep, HK, tq), F32), pltpu.VMEM((rep, HK, tq), F32),
                        pltpu.VMEM((rep, 8, tq), F32), pltpu.VMEM((rep, 8, tq), F32),
                        pltpu.VMEM((nq, rep, 1, tq), F32),
                        pltpu.VMEM((nq, rep, B_DH + ONES_ROWS, tq), F32)],
        compiler_params=_cparams(("parallel", "arbitrary")),
        name="gqa_attn",
    )(qt, k_tok, v_fm)


NAT_ROWS = 4


def _natten_kernel(q_ref, k_ref, v_ref, *rest, rows):
    bias_refs, o_ref = rest[:NAT_ROWS], rest[NAT_ROWS]
    nkeys = C_WIN_R * GRID_W
    npair = C_HEADS // 2
    own = (lax.broadcasted_iota(jnp.int32, (2 * GRID_W, 2 * C_DH), 0) // GRID_W
           == lax.broadcasted_iota(jnp.int32, (2 * GRID_W, 2 * C_DH), 1) // C_DH)
    starts = []
    for rr in range(NAT_ROWS):
        r = pl.program_id(1) * NAT_ROWS + rr
        rs = jnp.clip(r - C_WIN_R // 2, 0, rows - C_WIN_R)
        starts.append(pl.multiple_of(rs * GRID_W, GRID_W))
    units = [(rr, p) for rr in range(NAT_ROWS) for p in range(npair)]
    scores = []
    for rr, p in units:
        kp = k_ref[pl.ds(starts[rr], nkeys), p * 128:(p + 1) * 128]
        qp = q_ref[rr * GRID_W:(rr + 1) * GRID_W, p * 128:(p + 1) * 128].astype(F32)
        qm = jnp.where(own, jnp.concatenate([qp, qp], axis=0), 0.0).astype(BF16)
        s = lax.dot_general(qm, kp, (((1,), (1,)), ((), ())), preferred_element_type=F32)
        scores.append(s + bias_refs[rr][0, 2 * p:2 * p + 2].reshape(2 * GRID_W, nkeys))
    probs = []
    for s in scores:
        e = jnp.exp(s - jnp.max(s, axis=1, keepdims=True))
        probs.append((e.astype(BF16), jnp.sum(e, axis=1, keepdims=True)))
    outs = [[] for _ in range(NAT_ROWS)]
    for (rr, p), (e, l) in zip(units, probs):
        vp = v_ref[pl.ds(starts[rr], nkeys), p * 128:(p + 1) * 128]
        o = jnp.where(own, jnp.dot(e, vp, preferred_element_type=F32) / l, 0.0)
        outs[rr].append(o[:GRID_W] + o[GRID_W:])
    o_ref[...] = jnp.concatenate([jnp.concatenate(o, axis=1) for o in outs], axis=0).astype(o_ref.dtype)


def _natten_bias_table(rpb, rows):
    c = np.arange(GRID_W)[:, None]
    kc = np.arange(GRID_W)[None, :]
    cs = np.clip(c - C_WIN_C // 2, 0, GRID_W - C_WIN_C)
    valid = (kc >= cs) & (kc < cs + C_WIN_C)
    nd = 2 * C_WIN_C - 1
    onehot = (kc - c + (C_WIN_C - 1))[None] == np.arange(nd)[:, None, None]
    toe = jnp.sum(jnp.where(onehot[None, None], rpb.astype(F32)[:, :, :, None, None], 0.0), axis=2)
    toe = jnp.where(valid[None, None], toe, -1e30)
    tbl = jnp.stack([toe[:, C_WIN_R - 1 - v:2 * C_WIN_R - 1 - v] for v in range(C_WIN_R)], axis=0)
    return jnp.swapaxes(tbl, 2, 3).reshape(C_WIN_R, C_HEADS, GRID_W, C_WIN_R * GRID_W)


def _natten(main, bias_tbl, bsz, seq):
    t = bsz * seq
    rows = seq // GRID_W
    assert rows >= C_WIN_R
    half = C_WIN_R // 2

    def col(name):
        return _MAIN_OFF[name] // C_WIDTH

    cq, ck, cv = col("cq"), col("ck"), col("cv")
    assert rows % NAT_ROWS == 0
    steps = rows // NAT_ROWS

    def bias_spec(rr):
        def variant(b, i):
            r = i * NAT_ROWS + rr
            return (r - jnp.clip(r - half, 0, rows - C_WIN_R), 0, 0, 0)
        return pl.BlockSpec((1, C_HEADS, GRID_W, C_WIN_R * GRID_W), variant)

    return pl.pallas_call(
        functools.partial(_natten_kernel, rows=rows),
        out_shape=jax.ShapeDtypeStruct((t, C_WIDTH), BF16),
        grid=(bsz, steps),
        in_specs=[pl.BlockSpec((NAT_ROWS * GRID_W, C_WIDTH), lambda b, i: (b * steps + i, cq)),
                  pl.BlockSpec((seq, C_WIDTH), lambda b, i: (b, ck)),
                  pl.BlockSpec((seq, C_WIDTH), lambda b, i: (b, cv))]
                 + [bias_spec(rr) for rr in range(NAT_ROWS)],
        out_specs=pl.BlockSpec((NAT_ROWS * GRID_W, C_WIDTH), lambda b, i: (b * steps + i, 0)),
        compiler_params=_cparams(("parallel", "arbitrary")),
        name="natten",
    )(main, main, main, *([bias_tbl] * NAT_ROWS))


def _diff_attn_kernel(slopes_ref, q1_ref, q2_ref, k1_ref, k2_ref, v_ref, lq1_ref, lk1_ref, lq2_ref, lk2_ref,
                      sg_ref, o_ref, qpad_ref, e_ref, esl_ref, bias_ref, shift_ref, s0_ref, s1_ref, mx0_ref, mx1_ref,
                      m_ref, acc_ref, *, nkc, tq, lambda_init):
    pair = pl.program_id(1)
    nq = q1_ref.shape[0]
    row = lax.broadcasted_iota(jnp.int32, (2 * D_DH, tq), 0)

    def pad_queries(i, carry):
        for c, qr in enumerate((q1_ref, q2_ref)):
            qf = qr[i].astype(F32)
            for hh in range(2):
                qpad_ref[i, 2 * c + hh] = jnp.where((row // D_DH) == hh, qf, 0.0).astype(BF16)
        return carry

    lax.fori_loop(0, nq, pad_queries, 0)
    m_ref[...] = jnp.full(m_ref.shape, -jnp.inf, F32)
    acc_ref[...] = jnp.zeros(acc_ref.shape, F32)
    kk = lax.broadcasted_iota(jnp.int32, (KC, tq), 0)
    qq = lax.broadcasted_iota(jnp.int32, (KC, tq), 1)
    e = (kk - qq).astype(F32)
    e_ref[...] = e
    for hh in range(2):
        slope = slopes_ref[2 * pair + hh]
        esl_ref[0, hh] = e * slope
        esl_ref[1, hh] = e * (-slope)
    slots = ((s0_ref, mx0_ref), (s1_ref, mx1_ref))
    krefs = (k1_ref, k2_ref)

    def qk(kc, i, comp, hh):
        r0 = _aligned(kc * KC, KC)
        return jnp.dot(krefs[comp][pl.ds(r0, KC), :], qpad_ref[i, 2 * comp + hh], preferred_element_type=F32)

    def scores(c, i, half):
        s_ref, mx_ref = slots[half]
        kc = 2 * c + half
        off = lax.convert_element_type((kc - i) * KC, F32)
        dist = jnp.abs(e_ref[...] + off)
        for hh in range(2):
            bias_ref[half, hh] = dist * (-slopes_ref[2 * pair + hh])
        for hh in range(2):
            for comp in range(2):
                _stash_scores(qk(kc, i, comp, hh) + bias_ref[half, hh], s_ref, mx_ref, 2 * comp + hh)
                shift_ref[4 * half + 2 * comp + hh] = 0.0

    def scores_offdiag(c, i, half):
        s_ref, mx_ref = slots[half]
        kc = 2 * c + half
        off = lax.convert_element_type((kc - i) * KC, F32)
        after = jnp.where(kc > i, 1, 0)
        sign = jnp.where(kc > i, -1.0, 1.0)
        for hh in range(2):
            shift = sign * slopes_ref[2 * pair + hh] * off
            for comp in range(2):
                tile = 2 * comp + hh
                s = qk(kc, i, comp, hh) + esl_ref[after, hh]
                s_ref[tile] = s
                mx_ref[tile] = jnp.max(s.reshape(KC // 8, 8, tq), axis=0) + shift
                shift_ref[4 * half + tile] = shift

    def consume(c, i, half):
        s_ref, mx_ref = slots[half]
        for hh in range(2):
            vc1 = _with_ones(v_ref[2 * c + half, hh * D_DV:(hh + 1) * D_DV, :])
            for comp in range(2):
                tile = 2 * comp + hh
                _flash_update(s_ref, mx_ref, tile, vc1, m_ref, acc_ref, i, shift=shift_ref[4 * half + tile])

    _pipelined_sweep(nkc // 2, nq, scores, consume, scores_offdiag,
                     on_diagonal=lambda c, i, half: 2 * c + half == i, unroll=1)
    lam = (jnp.exp(jnp.sum(lq1_ref[...] * lk1_ref[...], axis=1, keepdims=True))
           - jnp.exp(jnp.sum(lq2_ref[...] * lk2_ref[...], axis=1, keepdims=True)) + lambda_init)

    def write_out(i, carry):
        outs = []
        for hh in range(2):
            o = _normalised(acc_ref[i, hh], D_DV) - lam * _normalised(acc_ref[i, 2 + hh], D_DV)
            ms = jnp.mean(o * o, axis=0, keepdims=True)
            outs.append(o * lax.rsqrt(ms + EPS) * sg_ref[...] * (1.0 - lambda_init))
        o_ref[pl.ds(pl.multiple_of(i * tq, tq), tq), :] = jnp.concatenate(outs, axis=0).T.astype(o_ref.dtype)
        return carry

    lax.fori_loop(0, nq, write_out, 0)


def _diff_attn(fm_d, main, lq1, lk1, lq2, lk2, subln_g, lambda_init, bsz, seq):
    t = bsz * seq
    tq = KC
    nq = seq // tq
    nkc = seq // KC
    pw = 2 * D_DH
    vw = 2 * D_DV
    dk0 = _MAIN_OFF["dk"] // pw
    vrow0 = (2 * D_HEADS * D_DH) // vw
    slopes = jnp.asarray([LOG2E * 2.0 ** (-8.0 * (h + 1) / D_HEADS) for h in range(D_HEADS)], F32)

    def qspec(c):
        return pl.BlockSpec((nq, pw, tq), lambda b, p: (b, 2 * c + p, 0))

    def kspec(c):
        return pl.BlockSpec((seq, pw), lambda b, p: (b, dk0 + 2 * c + p))

    vec = pl.BlockSpec((1, D_DH), lambda b, p: (0, 0))
    return pl.pallas_call(
        functools.partial(_diff_attn_kernel, nkc=nkc, tq=tq, lambda_init=lambda_init),
        out_shape=jax.ShapeDtypeStruct((t, D_WIDTH), BF16),
        grid=(bsz, D_HEADS // 2),
        in_specs=[pl.BlockSpec(memory_space=pltpu.SMEM),
                  qspec(0), qspec(1), kspec(0), kspec(1),
                  pl.BlockSpec((nkc, vw, KC), lambda b, p: (b, vrow0 + p, 0)),
                  vec, vec, vec, vec,
                  pl.BlockSpec((D_DV, 1), lambda b, p: (0, 0))],
        out_specs=pl.BlockSpec((seq, vw), lambda b, p: (b, p)),
        scratch_shapes=[pltpu.VMEM((nq, 4, 2 * D_DH, tq), BF16),
                        pltpu.VMEM((KC, tq), F32),
                        pltpu.VMEM((2, 2, KC, tq), F32),
                        pltpu.VMEM((2, 2, KC, tq), F32),
                        pltpu.SMEM((8,), F32),
                        pltpu.VMEM((4, KC, tq), F32), pltpu.VMEM((4, KC, tq), F32),
                        pltpu.VMEM((4, 8, tq), F32), pltpu.VMEM((4, 8, tq), F32),
                        pltpu.VMEM((nq, 4, 1, tq), F32),
                        pltpu.VMEM((nq, 4, D_DV + ONES_ROWS, tq), F32)],
        compiler_params=_cparams(("parallel", "arbitrary")),
        name="diff_attn",
    )(slopes, fm_d, fm_d, main, main, fm_d, lq1, lk1, lq2, lk2, subln_g)


def _merge_kernel(x_ref, gl_ref, ya_ref, yb_ref, yc_ref, yd_ref, wup_ref, wout_ref, o_ref):
    merged = None
    for g, y_ref in enumerate((ya_ref, yb_ref, yc_ref, yd_ref)):
        u = jnp.dot(y_ref[...], wup_ref[g], preferred_element_type=F32)
        gate = gl_ref[:, g * D_MODEL:(g + 1) * D_MODEL].astype(F32)
        term = _sigmoid(gate) * u
        merged = term if merged is None else merged + term
    o_ref[...] = x_ref[...] + jnp.dot(merged.astype(BF16), wout_ref[...], preferred_element_type=F32)


def _merge(x2, main, ya, yb, yc, yd, w_up, w_out):
    t = x2.shape[0]
    tm = min(512, t)
    ytile = pl.BlockSpec((tm, 512), lambda i: (i, 0))
    return pl.pallas_call(
        _merge_kernel,
        out_shape=jax.ShapeDtypeStruct((t, D_MODEL), F32),
        grid=(t // tm,),
        in_specs=[pl.BlockSpec((tm, D_MODEL), lambda i: (i, 0)),
                  pl.BlockSpec((tm, N_BRANCH * D_MODEL), lambda i: (i, 0)),
                  ytile, ytile, ytile, ytile,
                  pl.BlockSpec((N_BRANCH, 512, D_MODEL), lambda i: (0, 0, 0)),
                  pl.BlockSpec((D_MODEL, D_MODEL), lambda i: (0, 0))],
        out_specs=pl.BlockSpec((tm, D_MODEL), lambda i: (i, 0)),
        compiler_params=_cparams(("parallel",)),
        name="merge",
    )(x2, main, ya, yb, yc, yd, w_up, w_out)


def _ffn_kernel(x_ref, g_ref, wg_ref, wu_ref, wd_ref, fg_ref, o_ref, *, final):
    x = x_ref[...]
    ms = jnp.mean(x * x, axis=-1, keepdims=True)
    h = (x * lax.rsqrt(ms + EPS) * g_ref[...]).astype(BF16)
    acc = x
    for a, b in FF_SPLITS:
        gt = jnp.dot(h, wg_ref[:, a:b], preferred_element_type=F32)
        up = jnp.dot(h, wu_ref[:, a:b], preferred_element_type=F32)
        act = (gt * _sigmoid(gt)) * up
        acc = acc + jnp.dot(act.astype(BF16), wd_ref[a:b, :], preferred_element_type=F32)
    if final:
        ms2 = jnp.mean(acc * acc, axis=-1, keepdims=True)
        acc = acc * lax.rsqrt(ms2 + EPS) * fg_ref[...]
    o_ref[...] = acc


def _ffn(x2, g, wg, wu, wd, fg, final):
    t = x2.shape[0]
    tm = min(512, t)
    return pl.pallas_call(
        functools.partial(_ffn_kernel, final=final),
        out_shape=jax.ShapeDtypeStruct((t, D_MODEL), F32),
        grid=(t // tm,),
        in_specs=[pl.BlockSpec((tm, D_MODEL), lambda i: (i, 0)),
                  pl.BlockSpec((1, D_MODEL), lambda i: (0, 0)),
                  pl.BlockSpec((D_MODEL, D_FF), lambda i: (0, 0)),
                  pl.BlockSpec((D_MODEL, D_FF), lambda i: (0, 0)),
                  pl.BlockSpec((D_FF, D_MODEL), lambda i: (0, 0)),
                  pl.BlockSpec((1, D_MODEL), lambda i: (0, 0))],
        out_specs=pl.BlockSpec((tm, D_MODEL), lambda i: (i, 0)),
        compiler_params=_cparams(("parallel",)),
        name="ffn",
    )(x2, g, wg, wu, wd, fg)


def _rope_tables(seq):
    tpos = jnp.arange(seq)
    row = (tpos // GRID_W).astype(F32)
    colp = (tpos % GRID_W).astype(F32)
    n_freq = B_DH // 4
    inv = ROPE_THETA ** (-jnp.arange(n_freq, dtype=F32) / n_freq)
    ar = (row[:, None] * inv).T
    ac = (colp[:, None] * inv).T
    cos = jnp.concatenate([jnp.cos(ar), jnp.cos(ar), jnp.cos(ac), jnp.cos(ac)], axis=0)
    sin = jnp.concatenate([-jnp.sin(ar), jnp.sin(ar), -jnp.sin(ac), jnp.sin(ac)], axis=0)
    return cos.astype(F32), sin.astype(F32)


def _w_cols(w_in, name, scale=None):
    a, b = _OFF[name]
    w = w_in[:, a:b]
    return w if scale is None else w * scale


def _layer(x2, l, bsz, seq, cos, sin, norm1_g, w_in, a_conv_w, a_gate_bias, a_norm_g, b_qnorm_g, b_knorm_g,
           c_rpb, d_lq1, d_lk1, d_lq2, d_lk2, d_subln_g, w_up_a, w_up_b, w_up_c, w_up_d, w_out,
           norm2_g, w_ffn_gate, w_ffn_up, w_ffn_down, final_g, final):
    scales = {"cq": C_DH ** -0.5, "dq": D_DH ** -0.5 * LOG2E}
    w_main = jnp.concatenate([_w_cols(w_in, n, scales.get(n)) for n in _MAIN_ORDER], axis=1).astype(BF16)
    ga, _ = _OFF["ag"]
    ngroups, ngates = A_HEADS // A_HPS, 4 * A_HPS
    gate_cols = [ga + ty * A_HEADS + A_HPS * p + hh
                 for p in range(ngroups) for ty in range(4) for hh in range(A_HPS)]
    wg16 = w_in[:, jnp.asarray(gate_cols)]
    w_gate = jnp.pad(wg16.reshape(D_MODEL, ngroups, ngates), ((0, 0), (0, 0), (0, 128 - ngates)))
    w_gate = w_gate.reshape(D_MODEL, ngroups * 128).astype(BF16)
    gb = a_gate_bias[jnp.asarray([c - ga for c in gate_cols])].reshape(ngroups, ngates)
    gate_bias = jnp.pad(gb, ((0, 0), (0, 128 - ngates))).reshape(1, ngroups * 128).astype(F32)
    gate_bias_col = gb.reshape(ngroups * ngates, 1).astype(F32)
    wt_av = _w_cols(w_in, "av").T.astype(BF16)
    wt_b = jnp.concatenate([_w_cols(w_in, n) for n in ("bq", "bk", "bv")], axis=1).T.astype(BF16)
    wt_d = jnp.concatenate([_w_cols(w_in, "dq", scales["dq"]), _w_cols(w_in, "dv")], axis=1).T.astype(BF16)
    cw = a_conv_w.astype(F32)
    gw_a = A_HPS * A_DH
    conv_w = jnp.stack([jnp.concatenate([cw[:, p * gw_a:(p + 1) * gw_a],
                                         cw[:, A_WIDTH + p * gw_a:A_WIDTH + (p + 1) * gw_a]], axis=1)
                        for p in range(ngroups)], axis=0)

    main, gates, h = _proj(x2, norm1_g.reshape(1, D_MODEL).astype(F32), w_main, w_gate)
    fm_b = _projt(h, wt_b, F32, "proj_fm_b")
    fm_d = _projt(h, wt_d, BF16, "proj_fm_d")

    vt_a = _projt(h, wt_av, BF16, "proj_fm_av", chunk=A_CHUNK)
    gates_t = _projt(h, wg16.T.astype(BF16), F32, "proj_gates_t", chunk=A_CHUNK)
    y_a = _mlstm(main, vt_a, gates, gates_t, gate_bias, gate_bias_col, conv_w,
                 a_norm_g.reshape(1, A_WIDTH).astype(F32), bsz, seq)

    qt_b, k_b, v_b = _gqa_prep(fm_b, cos, sin, b_qnorm_g.reshape(B_DH, 1).astype(F32),
                               b_knorm_g.reshape(B_DH, 1).astype(F32), seq)
    y_b = _gqa_attn(qt_b, k_b, v_b, bsz, seq)

    y_c = _natten(main, _natten_bias_table(c_rpb, seq // GRID_W), bsz, seq)

    lambda_init = 0.8 - 0.6 * math.exp(-0.3 * l)
    vec = lambda a: a.reshape(1, D_DH).astype(F32)
    y_d = _diff_attn(fm_d, main, vec(d_lq1), vec(d_lk1), vec(d_lq2), vec(d_lk2),
                     d_subln_g.reshape(D_DV, 1).astype(F32), lambda_init, bsz, seq)

    w_up = jnp.stack([w_up_a, w_up_b, w_up_c, w_up_d], axis=0).astype(BF16)
    x2 = _merge(x2, main, y_a, y_b, y_c, y_d, w_up, w_out.astype(BF16))
    return _ffn(x2, norm2_g.reshape(1, D_MODEL).astype(F32), w_ffn_gate.astype(BF16), w_ffn_up.astype(BF16),
                w_ffn_down.astype(BF16), final_g.reshape(1, D_MODEL).astype(F32), final)


def kernel(x, norm1_g, w_in, a_conv_w, a_gate_bias, a_norm_g, b_qnorm_g, b_knorm_g, c_rpb, d_lambda_q1, d_lambda_k1, d_lambda_q2, d_lambda_k2, d_subln_g, w_up_a, w_up_b, w_up_c, w_up_d, w_out, norm2_g, w_ffn_gate, w_ffn_up, w_ffn_down, final_g):
    bsz, seq, _ = x.shape
    assert seq % KC == 0 and seq % GRID_W == 0
    cos, sin = _rope_tables(seq)
    x2 = x.reshape(bsz * seq, D_MODEL)
    depth = norm1_g.shape[0]
    for l in range(depth):
        x2 = _layer(x2, l, bsz, seq, cos, sin, norm1_g[l], w_in[l], a_conv_w[l], a_gate_bias[l], a_norm_g[l],
                    b_qnorm_g[l], b_knorm_g[l], c_rpb[l], d_lambda_q1[l], d_lambda_k1[l], d_lambda_q2[l],
                    d_lambda_k2[l], d_subln_g[l], w_up_a[l], w_up_b[l], w_up_c[l], w_up_d[l], w_out[l],
                    norm2_g[l], w_ffn_gate[l], w_ffn_up[l], w_ffn_down[l], final_g, l == depth - 1)
    return x2.reshape(bsz, seq, D_MODEL)
```

```python
import functools
import math

import jax
import jax.numpy as jnp
import numpy as np
from jax import lax
from jax.experimental import pallas as pl
from jax.experimental.pallas import tpu as pltpu

F32 = jnp.float32
BF16 = jnp.bfloat16

D_MODEL = 1024
DEPTH = 2
GRID_W = 64
EPS = 1e-6
N_BRANCH = 4
A_HEADS, A_DH, A_CHUNK = 4, 128, 128
A_WIDTH = A_HEADS * A_DH
B_HEADS, B_KV, B_DH = 8, 2, 64
B_WIDTH = B_HEADS * B_DH
ROPE_THETA = 10000.0
C_HEADS, C_DH, C_WIN_R, C_WIN_C = 8, 64, 8, 16
C_WIDTH = C_HEADS * C_DH
D_HEADS, D_DH = 4, 64
D_DV = 2 * D_DH
D_WIDTH = D_HEADS * D_DV
D_FF = ((8 * D_MODEL + 3 * 256 - 1) // (3 * 256)) * 256
FF_SPLITS = ((0, 1536), (1536, D_FF))

_SIZES = (A_WIDTH, A_WIDTH, A_WIDTH, A_WIDTH, 4 * A_HEADS, B_WIDTH, B_KV * B_DH, B_KV * B_DH,
          C_WIDTH, C_WIDTH, C_WIDTH, 2 * D_HEADS * D_DH, 2 * D_HEADS * D_DH, D_WIDTH, N_BRANCH * D_MODEL)
_NAMES = ("aq", "ak", "av", "ao", "ag", "bq", "bk", "bv", "cq", "ck", "cv", "dq", "dk", "dv", "gl")
_OFF = {}
_o = 0
for _n, _s in zip(_NAMES, _SIZES):
    _OFF[_n] = (_o, _o + _s)
    _o += _s

_MAIN_ORDER = ("gl", "aq", "ak", "ao", "cq", "ck", "cv", "dk")
_MAIN_OFF = {}
_o = 0
for _n in _MAIN_ORDER:
    _MAIN_OFF[_n] = _o
    _o += _OFF[_n][1] - _OFF[_n][0]
MAIN_N = _o

KC = 512
VMEM_LIMIT = 56 * 1024 * 1024


def _cparams(sem, vmem=VMEM_LIMIT):
    return pltpu.CompilerParams(dimension_semantics=sem, vmem_limit_bytes=vmem)


def _sigmoid(x):
    return 1.0 / (1.0 + jnp.exp(-x))


def _aligned(x, m):
    return x if isinstance(x, int) else pl.multiple_of(x, m)


def _proj_kernel(x_ref, g_ref, w_ref, wg_ref, o_ref, og_ref, h_ref):
    @pl.when(pl.program_id(1) == 0)
    def _():
        x = x_ref[...]
        ms = jnp.mean(x * x, axis=-1, keepdims=True)
        hb = (x * lax.rsqrt(ms + EPS) * g_ref[...]).astype(BF16)
        h_ref[...] = hb
        og_ref[...] = jnp.dot(hb, wg_ref[...], preferred_element_type=F32)

    o_ref[...] = jnp.dot(h_ref[...], w_ref[...], preferred_element_type=F32).astype(o_ref.dtype)


def _proj(x2, g, w_main, w_gate):
    t = x2.shape[0]
    tm = min(2048, t)
    n = w_main.shape[1]
    tn = 768 if n % 768 == 0 else 1024
    ng = w_gate.shape[1]
    return pl.pallas_call(
        _proj_kernel,
        out_shape=(jax.ShapeDtypeStruct((t, n), BF16),
                   jax.ShapeDtypeStruct((t, ng), F32),
                   jax.ShapeDtypeStruct((t, D_MODEL), BF16)),
        grid=(t // tm, n // tn),
        in_specs=[pl.BlockSpec((tm, D_MODEL), lambda i, j: (i, 0)),
                  pl.BlockSpec((1, D_MODEL), lambda i, j: (0, 0)),
                  pl.BlockSpec((D_MODEL, tn), lambda i, j: (0, j)),
                  pl.BlockSpec((D_MODEL, ng), lambda i, j: (0, 0))],
        out_specs=(pl.BlockSpec((tm, tn), lambda i, j: (i, j)),
                   pl.BlockSpec((tm, ng), lambda i, j: (i, 0)),
                   pl.BlockSpec((tm, D_MODEL), lambda i, j: (i, 0))),
        compiler_params=_cparams(("parallel", "arbitrary")),
        name="proj_main",
    )(x2, g, w_main, w_gate)


def _projt_kernel(h_ref, wt_ref, o_ref, *, nchunk, chunk):
    for c in range(nchunk):
        hc = h_ref[c * chunk:(c + 1) * chunk, :]
        o_ref[c] = lax.dot_general(wt_ref[...], hc, (((1,), (1,)), ((), ())),
                                   preferred_element_type=F32).astype(o_ref.dtype)


def _projt(h, wt, out_dtype, name, chunk=KC):
    t = h.shape[0]
    n = wt.shape[0]
    tm = min(2048, t)
    tn = n if n <= 768 else (768 if n % 768 == 0 else 512)
    nchunk = tm // chunk
    return pl.pallas_call(
        functools.partial(_projt_kernel, nchunk=nchunk, chunk=chunk),
        out_shape=jax.ShapeDtypeStruct((t // chunk, n, chunk), out_dtype),
        grid=(t // tm, n // tn),
        in_specs=[pl.BlockSpec((tm, D_MODEL), lambda i, j: (i, 0)),
                  pl.BlockSpec((tn, D_MODEL), lambda i, j: (j, 0))],
        out_specs=pl.BlockSpec((nchunk, tn, chunk), lambda i, j: (i, j, 0)),
        compiler_params=_cparams(("parallel", "arbitrary")),
        name=name,
    )(h, wt)


def _log_sigmoid(x):
    return jnp.minimum(x, 0.0) - jnp.log1p(jnp.exp(-jnp.abs(x)))


def _tri_dot(mat, x):
    hi = x.astype(BF16)
    r1 = x - hi.astype(F32)
    mid = r1.astype(BF16)
    lo = (r1 - mid.astype(F32)).astype(BF16)
    return (jnp.dot(mat, hi, preferred_element_type=F32)
            + jnp.dot(mat, mid, preferred_element_type=F32)
            + jnp.dot(mat, lo, preferred_element_type=F32))


def _split3(x):
    hi = x.astype(BF16)
    r1 = x - hi.astype(F32)
    mid = r1.astype(BF16)
    return hi, mid, (r1 - mid.astype(F32)).astype(BF16)


def _tri_dot_r(x, mat):
    return sum(jnp.dot(piece, mat, preferred_element_type=F32) for piece in _split3(x))


NROWS = 8
A_HPS = 4


def _mlstm_kernel(q_ref, k_ref, vt_ref, o_ref, g_ref, gt_ref, gb_ref, gbc_ref, cw_ref, ng_ref, y_ref,
                  qs_ref, ks_ref, hs_ref, st_ref, m_ref, *, seq):
    L = A_CHUNK
    nc = seq // L
    hw = A_HPS * A_DH

    rowi = lax.broadcasted_iota(jnp.int32, (L, hw), 0)

    def conv_body(c, carry):
        r0 = pl.multiple_of(c * L, L)
        pstart = pl.multiple_of(jnp.maximum(r0 - 16, 0), 16)
        nstart = pl.multiple_of(jnp.minimum(r0 + L, seq - 16), 16)
        has_prev = jnp.where(c > 0, 1.0, 0.0)
        has_next = jnp.where(c < nc - 1, 1.0, 0.0)
        for src, dst, woff, scale in ((q_ref, qs_ref, 0, 1.0), (k_ref, ks_ref, hw, A_DH ** -0.5)):
            xc = src[pl.ds(r0, L), :].astype(F32)
            prev = src[pl.ds(pstart, 16), :].astype(F32)[15:16] * has_prev
            nxt = src[pl.ds(nstart, 16), :].astype(F32)[0:1] * has_next
            xp = jnp.where(rowi == 0, prev, pltpu.roll(xc, 1, 0))
            xn = jnp.where(rowi == L - 1, nxt, pltpu.roll(xc, L - 1, 0))
            w = cw_ref[:, woff:woff + hw]
            y = xp * w[0:1] + xc * w[1:2] + xn * w[2:3]
            y = y * _sigmoid(y)
            dst[pl.ds(r0, L), :] = (y * scale).astype(BF16)
        return carry

    lax.fori_loop(0, nc, conv_body, 0)

    st_ref[...] = jnp.zeros(st_ref.shape, F32)
    m_ref[...] = jnp.zeros(m_ref.shape, F32)

    ti = lax.broadcasted_iota(jnp.int32, (L, L), 0)
    tj = lax.broadcasted_iota(jnp.int32, (L, L), 1)
    lower = tj <= ti
    upper = tj >= ti
    lmat = jnp.where(lower, 1.0, 0.0).astype(BF16)
    umat = jnp.where(upper, 1.0, 0.0).astype(BF16)
    nt = (((1,), (1,)), ((), ()))

    def step(cf, cb):
        chains = []
        for d, c in ((0, cf), (1, cb)):
            r0 = pl.multiple_of(c * L, L)
            gcol = g_ref[pl.ds(r0, L), :] + gb_ref[...]
            bcol_all = _tri_dot(lmat if d == 0 else umat, _log_sigmoid(gcol))
            grow = gt_ref[c] + gbc_ref[...]
            brow_all = _tri_dot_r(_log_sigmoid(grow), umat if d == 0 else lmat)
            for hh in range(A_HPS):
                il = 2 * A_HPS * d + hh
                fl = il + A_HPS
                ch = dict(d=d, idx=A_HPS * d + hh, c=c)
                ch["acol"] = gcol[:, il:il + 1] - bcol_all[:, fl:fl + 1]
                ch["irow"] = grow[il:il + 1, :]
                ch["brow"] = brow_all[fl:fl + 1, :]
                ch["q"] = qs_ref[pl.ds(r0, L), hh * A_DH:(hh + 1) * A_DH]
                ch["k"] = ks_ref[pl.ds(r0, L), hh * A_DH:(hh + 1) * A_DH]
                ch["vt"] = vt_ref[c, hh * A_DH:(hh + 1) * A_DH, :]
                ch["state"] = st_ref[ch["idx"]]
                ch["st"] = lax.dot_general(ch["k"], ch["q"], nt, preferred_element_type=F32)
                ch["it"] = lax.dot_general(ch["state"].astype(BF16), ch["q"], nt,
                                           preferred_element_type=F32)
                chains.append(ch)
        for ch in chains:
            d, brow = ch["d"], ch["brow"]
            mask = upper if d == 0 else lower
            m_prev = m_ref[ch["idx"]][:, 0:1]
            dmt = jnp.where(mask, brow + ch["acol"], -jnp.inf)
            inter = brow + m_prev
            mt = jnp.maximum(inter, jnp.max(dmt, axis=0, keepdims=True))
            ch["w_inter"] = jnp.exp(inter - mt)
            ch["floor"] = jnp.exp(-mt)
            sqk = ch["st"] * jnp.exp(dmt - mt)
            ch["sqk_sum"] = jnp.sum(sqk, axis=0, keepdims=True)
            ch["sqk"] = sqk.astype(BF16)
            bl = brow[:, L - 1:L] if d == 0 else brow[:, 0:1]
            gvec = bl - brow + ch["irow"]
            m_new = jnp.maximum(bl + m_prev, jnp.max(gvec, axis=1, keepdims=True))
            ch["wc"] = jnp.exp(bl + m_prev - m_new)
            ws = jnp.exp(gvec - m_new)
            ch["lhs"] = jnp.concatenate([ch["vt"].astype(F32) * ws, jnp.broadcast_to(ws, (NROWS, L))],
                                        axis=0).astype(BF16)
            m_ref[ch["idx"]] = jnp.broadcast_to(m_new, (1, 128))
        for ch in chains:
            ch["pv"] = jnp.dot(ch["vt"], ch["sqk"], preferred_element_type=F32)
            ch["upd"] = jnp.dot(ch["lhs"], ch["k"], preferred_element_type=F32)
        outs = {0: [], 1: []}
        for ch in chains:
            num = ch["w_inter"] * ch["it"][:A_DH] + ch["pv"]
            den = ch["w_inter"] * ch["it"][A_DH:A_DH + 1] + ch["sqk_sum"]
            outs[ch["d"]].append(num / jnp.maximum(jnp.abs(den), ch["floor"]))
            st_ref[ch["idx"]] = ch["wc"] * ch["state"] + ch["upd"]
        return outs

    def finalize(tots, r0):
        parts = []
        for tot in tots:
            mu = jnp.mean(tot, axis=0, keepdims=True)
            cen = tot - mu
            var = jnp.mean(cen * cen, axis=0, keepdims=True)
            parts.append((cen * lax.rsqrt(var + EPS)).T)
        hn = jnp.concatenate(parts, axis=1) * ng_ref[...]
        y_ref[pl.ds(r0, L), :] = (hn * _sigmoid(o_ref[pl.ds(r0, L), :].astype(F32))).astype(y_ref.dtype)

    def first_half(j, carry):
        outs = step(j, nc - 1 - j)
        for d, c in ((0, j), (1, nc - 1 - j)):
            for hh in range(A_HPS):
                hs_ref[c, hh] = outs[d][hh]
        return carry

    def second_half(j, carry):
        outs = step(j, nc - 1 - j)
        for d, c in ((0, j), (1, nc - 1 - j)):
            finalize([hs_ref[c, hh] + outs[d][hh] for hh in range(A_HPS)], pl.multiple_of(c * L, L))
        return carry

    lax.fori_loop(0, nc // 2, first_half, 0)
    lax.fori_loop(nc // 2, nc, second_half, 0)


def _mlstm(main, vt, gates, gates_t, gate_bias, gate_bias_col, conv_w, norm_g, bsz, seq):
    t = bsz * seq
    hw = A_HPS * A_DH
    nc = seq // A_CHUNK
    ngates = 4 * A_HPS
    assert nc % 2 == 0
    once = pl.Buffered(1)

    def col(name):
        base = _MAIN_OFF[name] // hw
        return pl.BlockSpec((seq, hw), lambda b, p: (b, base + p), pipeline_mode=once)

    return pl.pallas_call(
        functools.partial(_mlstm_kernel, seq=seq),
        out_shape=jax.ShapeDtypeStruct((t, A_WIDTH), BF16),
        grid=(bsz, A_HEADS // A_HPS),
        in_specs=[col("aq"), col("ak"),
                  pl.BlockSpec((nc, hw, A_CHUNK), lambda b, p: (b, p, 0), pipeline_mode=once),
                  col("ao"),
                  pl.BlockSpec((seq, 128), lambda b, p: (b, p), pipeline_mode=once),
                  pl.BlockSpec((nc, ngates, A_CHUNK), lambda b, p: (b, p, 0), pipeline_mode=once),
                  pl.BlockSpec((1, 128), lambda b, p: (0, p)),
                  pl.BlockSpec((ngates, 1), lambda b, p: (p, 0)),
                  pl.BlockSpec((None, 3, 2 * hw), lambda b, p: (p, 0, 0)),
                  pl.BlockSpec((1, hw), lambda b, p: (0, p))],
        out_specs=pl.BlockSpec((seq, hw), lambda b, p: (b, p)),
        scratch_shapes=[pltpu.VMEM((seq, hw), BF16), pltpu.VMEM((seq, hw), BF16),
                        pltpu.VMEM((nc, A_HPS, A_DH, A_CHUNK), F32),
                        pltpu.VMEM((2 * A_HPS, A_DH + NROWS, A_DH), F32),
                        pltpu.VMEM((2 * A_HPS, 1, 128), F32)],
        compiler_params=_cparams(("parallel", "arbitrary")),
        name="mlstm",
    )(main, main, vt, main, gates, gates_t, gate_bias, gate_bias_col, conv_w, norm_g)


LOG2E = 1.4426950408889634


def _stash_scores(s, s_ref, mx_ref, idx):
    tk, tq = s.shape
    s_ref[idx] = s
    mx_ref[idx] = jnp.max(s.reshape(tk // 8, 8, tq), axis=0)


ONES_ROWS = 8


def _with_ones(vc):
    return jnp.concatenate([vc, jnp.ones((ONES_ROWS, vc.shape[1]), vc.dtype)], axis=0)


def _flash_update(s_ref, mx_ref, tile, vc1, m_ref, acc_ref, qb, shift=None):
    m_old = m_ref[qb, tile]
    m_new = jnp.maximum(m_old, jnp.max(mx_ref[tile], axis=0, keepdims=True))
    alpha = jnp.exp2(m_old - m_new)
    p = jnp.exp2(s_ref[tile] - (m_new if shift is None else m_new - shift))
    acc_ref[qb, tile] = alpha * acc_ref[qb, tile] + jnp.dot(vc1, p.astype(BF16), preferred_element_type=F32)
    m_ref[qb, tile] = m_new


def _normalised(acc, dv):
    return acc[:dv] / acc[dv:dv + 1]


HK = KC // 2


def _pipelined_sweep(nkc, nq, scores, consume, scores_offdiag=None, on_diagonal=None, unroll=2):
    total = nkc * nq

    def pair_of(it):
        return it // nq, it % nq

    def step(score_fn, it):
        c, i = pair_of(it)
        c2, i2 = pair_of(it + 1)
        score_fn(c, i, 1)
        consume(c, i, 0)
        score_fn(c2, i2, 0)
        consume(c, i, 1)

    def touches_diagonal(it):
        c, i = pair_of(it)
        c2, i2 = pair_of(it + 1)
        return on_diagonal(c, i, 1) | on_diagonal(c2, i2, 0)

    scores(0, 0, 0)
    trips = (total - 1) // unroll

    def body(j, carry):
        its = [j * unroll + u for u in range(unroll)]

        def trip(score_fn):
            def run():
                for it in its:
                    step(score_fn, it)
            return run

        if scores_offdiag is None:
            trip(scores)()
        else:
            slow = touches_diagonal(its[0])
            for it in its[1:]:
                slow = slow | touches_diagonal(it)
            lax.cond(slow, trip(scores), trip(scores_offdiag))
        return carry

    lax.fori_loop(0, trips, body, 0)
    for it in range(trips * unroll, total - 1):
        step(scores, it)
    scores(nkc - 1, nq - 1, 1)
    consume(nkc - 1, nq - 1, 0)
    consume(nkc - 1, nq - 1, 1)


def _gqa_proj_kernel(h_ref, wt_ref, cos_ref, sin_ref, qg_ref, kg_ref, q_ref, k_ref, v_ref, *, nchunk):
    def norm_rope(xh, g, cos, sin):
        ms = jnp.mean(xh * xh, axis=0, keepdims=True)
        xn = xh * lax.rsqrt(ms + EPS) * g
        partner = jnp.concatenate([xn[16:32], xn[0:16], xn[48:64], xn[32:48]], axis=0)
        return xn * cos + partner * sin

    for c in range(nchunk):
        tok = slice(c * KC, (c + 1) * KC)
        x = lax.dot_general(wt_ref[...], h_ref[tok, :], (((1,), (1,)), ((), ())), preferred_element_type=F32)
        cos, sin = cos_ref[:, tok], sin_ref[:, tok]
        for h in range(B_HEADS):
            qh = norm_rope(x[h * B_DH:(h + 1) * B_DH], qg_ref[...], cos, sin) * (B_DH ** -0.5 * LOG2E)
            q_ref[c, h * B_DH:(h + 1) * B_DH, :] = qh.astype(BF16)
        ks = [norm_rope(x[B_WIDTH + g * B_DH:B_WIDTH + (g + 1) * B_DH], kg_ref[...], cos, sin)
              for g in range(B_KV)]
        k_ref[tok, :] = jnp.concatenate(ks, axis=0).T.astype(BF16)
        v_ref[c] = x[B_WIDTH + B_KV * B_DH:].astype(BF16)


def _gqa_proj(h, wt_b, cos, sin, qg, kg, seq):
    t = h.shape[0]
    tm = min(2048, seq)
    nchunk = tm // KC
    spt = seq // tm
    kvw = B_KV * B_DH
    nrows = B_WIDTH + 2 * kvw
    return pl.pallas_call(
        functools.partial(_gqa_proj_kernel, nchunk=nchunk),
        out_shape=(jax.ShapeDtypeStruct((t // KC, B_WIDTH, KC), BF16),
                   jax.ShapeDtypeStruct((t, kvw), BF16),
                   jax.ShapeDtypeStruct((t // KC, kvw, KC), BF16)),
        grid=(t // tm,),
        in_specs=[pl.BlockSpec((tm, D_MODEL), lambda i: (i, 0)),
                  pl.BlockSpec((nrows, D_MODEL), lambda i: (0, 0)),
                  pl.BlockSpec((B_DH, tm), lambda i: (0, i % spt)),
                  pl.BlockSpec((B_DH, tm), lambda i: (0, i % spt)),
                  pl.BlockSpec((B_DH, 1), lambda i: (0, 0)),
                  pl.BlockSpec((B_DH, 1), lambda i: (0, 0))],
        out_specs=(pl.BlockSpec((nchunk, B_WIDTH, KC), lambda i: (i, 0, 0)),
                   pl.BlockSpec((tm, kvw), lambda i: (i, 0)),
                   pl.BlockSpec((nchunk, kvw, KC), lambda i: (i, 0, 0))),
        compiler_params=_cparams(("parallel",)),
        name="gqa_proj",
    )(h, wt_b, cos, sin, qg, kg)


def _gqa_attn_kernel(q_ref, k_ref, v_ref, o_ref, qpad_ref, s0_ref, s1_ref, mx0_ref, mx1_ref,
                     m_ref, acc_ref, *, nkc, tq):
    g = pl.program_id(1)
    rep = B_HEADS // B_KV
    nq = q_ref.shape[0]
    row = lax.broadcasted_iota(jnp.int32, (2 * B_DH, tq), 0)
    sel = (row // B_DH) == g

    def pad_queries(i, carry):
        for r in range(rep):
            qh = q_ref[i, r * B_DH:(r + 1) * B_DH, :].astype(F32)
            qpad_ref[i, r] = jnp.where(sel, jnp.concatenate([qh, qh], axis=0), 0.0).astype(BF16)
        return carry

    lax.fori_loop(0, nq, pad_queries, 0)
    m_ref[...] = jnp.full(m_ref.shape, -jnp.inf, F32)
    acc_ref[...] = jnp.zeros(acc_ref.shape, F32)
    slots = ((s0_ref, mx0_ref), (s1_ref, mx1_ref))

    def scores(c, i, half):
        s_ref, mx_ref = slots[half]
        kc = k_ref[pl.ds(_aligned(c * KC + half * HK, HK), HK), :]
        for r in range(rep):
            _stash_scores(jnp.dot(kc, qpad_ref[i, r], preferred_element_type=F32), s_ref, mx_ref, r)

    def consume(c, i, half):
        s_ref, mx_ref = slots[half]
        vc1 = _with_ones(v_ref[c, :, half * HK:(half + 1) * HK])
        for r in range(rep):
            _flash_update(s_ref, mx_ref, r, vc1, m_ref, acc_ref, i)

    _pipelined_sweep(nkc, nq, scores, consume, unroll=4)

    def write_out(i, carry):
        outs = [_normalised(acc_ref[i, r], B_DH) for r in range(rep)]
        o_ref[pl.ds(pl.multiple_of(i * tq, tq), tq), :] = jnp.concatenate(outs, axis=0).T.astype(o_ref.dtype)
        return carry

    lax.fori_loop(0, nq, write_out, 0)


def _gqa_attn(qt, k_tok, v_fm, bsz, seq):
    t = bsz * seq
    tq = KC
    nq = seq // tq
    nkc = seq // KC
    rep = B_HEADS // B_KV
    gw = rep * B_DH
    return pl.pallas_call(
        functools.partial(_gqa_attn_kernel, nkc=nkc, tq=tq),
        out_shape=jax.ShapeDtypeStruct((t, B_WIDTH), BF16),
        grid=(bsz, B_KV),
        in_specs=[pl.BlockSpec((nq, gw, tq), lambda b, g: (b, g, 0)),
                  pl.BlockSpec((seq, B_KV * B_DH), lambda b, g: (b, 0)),
                  pl.BlockSpec((nkc, B_DH, KC), lambda b, g: (b, g, 0))],
        out_specs=pl.BlockSpec((seq, gw), lambda b, g: (b, g)),
        scratch_shapes=[pltpu.VMEM((nq, rep, 2 * B_DH, tq), BF16),
                        pltpu.VMEM((rep, HK, tq), F32), pltpu.VMEM((rep, HK, tq), F32),
                        pltpu.VMEM((rep, 8, tq), F32), pltpu.VMEM((rep, 8, tq), F32),
                        pltpu.VMEM((nq, rep, 1, tq), F32),
                        pltpu.VMEM((nq, rep, B_DH + ONES_ROWS, tq), F32)],
        compiler_params=_cparams(("parallel", "arbitrary")),
        name="gqa_attn",
    )(qt, k_tok, v_fm)


NAT_ROWS = 4


def _natten_kernel(q_ref, k_ref, v_ref, *rest, rows):
    bias_refs, o_ref = rest[:NAT_ROWS], rest[NAT_ROWS]
    nkeys = C_WIN_R * GRID_W
    npair = C_HEADS // 2
    own = (lax.broadcasted_iota(jnp.int32, (2 * GRID_W, 2 * C_DH), 0) // GRID_W
           == lax.broadcasted_iota(jnp.int32, (2 * GRID_W, 2 * C_DH), 1) // C_DH)
    starts = []
    for rr in range(NAT_ROWS):
        r = pl.program_id(1) * NAT_ROWS + rr
        rs = jnp.clip(r - C_WIN_R // 2, 0, rows - C_WIN_R)
        starts.append(pl.multiple_of(rs * GRID_W, GRID_W))
    units = [(rr, p) for rr in range(NAT_ROWS) for p in range(npair)]
    scores = []
    for rr, p in units:
        kp = k_ref[pl.ds(starts[rr], nkeys), p * 128:(p + 1) * 128]
        qp = q_ref[rr * GRID_W:(rr + 1) * GRID_W, p * 128:(p + 1) * 128].astype(F32)
        qm = jnp.where(own, jnp.concatenate([qp, qp], axis=0), 0.0).astype(BF16)
        s = lax.dot_general(qm, kp, (((1,), (1,)), ((), ())), preferred_element_type=F32)
        scores.append(s + bias_refs[rr][0, 2 * p:2 * p + 2].reshape(2 * GRID_W, nkeys))
    probs = []
    for s in scores:
        e = jnp.exp(s - jnp.max(s, axis=1, keepdims=True))
        probs.append((e.astype(BF16), jnp.sum(e, axis=1, keepdims=True)))
    outs = [[] for _ in range(NAT_ROWS)]
    for (rr, p), (e, l) in zip(units, probs):
        vp = v_ref[pl.ds(starts[rr], nkeys), p * 128:(p + 1) * 128]
        o = jnp.where(own, jnp.dot(e, vp, preferred_element_type=F32) / l, 0.0)
        outs[rr].append(o[:GRID_W] + o[GRID_W:])
    o_ref[...] = jnp.concatenate([jnp.concatenate(o, axis=1) for o in outs], axis=0).astype(o_ref.dtype)


def _natten_bias_table(rpb, rows):
    c = np.arange(GRID_W)[:, None]
    kc = np.arange(GRID_W)[None, :]
    cs = np.clip(c - C_WIN_C // 2, 0, GRID_W - C_WIN_C)
    valid = (kc >= cs) & (kc < cs + C_WIN_C)
    nd = 2 * C_WIN_C - 1
    onehot = (kc - c + (C_WIN_C - 1))[None] == np.arange(nd)[:, None, None]
    toe = jnp.sum(jnp.where(onehot[None, None], rpb.astype(F32)[:, :, :, None, None], 0.0), axis=2)
    toe = jnp.where(valid[None, None], toe, -1e30)
    tbl = jnp.stack([toe[:, C_WIN_R - 1 - v:2 * C_WIN_R - 1 - v] for v in range(C_WIN_R)], axis=0)
    return jnp.swapaxes(tbl, 2, 3).reshape(C_WIN_R, C_HEADS, GRID_W, C_WIN_R * GRID_W)


def _natten(main, bias_tbl, bsz, seq):
    t = bsz * seq
    rows = seq // GRID_W
    assert rows >= C_WIN_R
    half = C_WIN_R // 2

    def col(name):
        return _MAIN_OFF[name] // C_WIDTH

    cq, ck, cv = col("cq"), col("ck"), col("cv")
    assert rows % NAT_ROWS == 0
    steps = rows // NAT_ROWS

    def bias_spec(rr):
        def variant(b, i):
            r = i * NAT_ROWS + rr
            return (r - jnp.clip(r - half, 0, rows - C_WIN_R), 0, 0, 0)
        return pl.BlockSpec((1, C_HEADS, GRID_W, C_WIN_R * GRID_W), variant)

    return pl.pallas_call(
        functools.partial(_natten_kernel, rows=rows),
        out_shape=jax.ShapeDtypeStruct((t, C_WIDTH), BF16),
        grid=(bsz, steps),
        in_specs=[pl.BlockSpec((NAT_ROWS * GRID_W, C_WIDTH), lambda b, i: (b * steps + i, cq)),
                  pl.BlockSpec((seq, C_WIDTH), lambda b, i: (b, ck)),
                  pl.BlockSpec((seq, C_WIDTH), lambda b, i: (b, cv))]
                 + [bias_spec(rr) for rr in range(NAT_ROWS)],
        out_specs=pl.BlockSpec((NAT_ROWS * GRID_W, C_WIDTH), lambda b, i: (b * steps + i, 0)),
        compiler_params=_cparams(("parallel", "arbitrary")),
        name="natten",
    )(main, main, main, *([bias_tbl] * NAT_ROWS))


def _diff_attn_kernel(slopes_ref, q1_ref, q2_ref, k1_ref, k2_ref, v_ref, lq1_ref, lk1_ref, lq2_ref, lk2_ref,
                      sg_ref, o_ref, qpad_ref, e_ref, esl_ref, bias_ref, shift_ref, s0_ref, s1_ref, mx0_ref, mx1_ref,
                      m_ref, acc_ref, *, nkc, tq, lambda_init):
    pair = pl.program_id(1)
    nq = q1_ref.shape[0]
    row = lax.broadcasted_iota(jnp.int32, (2 * D_DH, tq), 0)

    def pad_queries(i, carry):
        for c, qr in enumerate((q1_ref, q2_ref)):
            qf = qr[i].astype(F32)
            for hh in range(2):
                qpad_ref[i, 2 * c + hh] = jnp.where((row // D_DH) == hh, qf, 0.0).astype(BF16)
        return carry

    lax.fori_loop(0, nq, pad_queries, 0)
    m_ref[...] = jnp.full(m_ref.shape, -jnp.inf, F32)
    acc_ref[...] = jnp.zeros(acc_ref.shape, F32)
    kk = lax.broadcasted_iota(jnp.int32, (KC, tq), 0)
    qq = lax.broadcasted_iota(jnp.int32, (KC, tq), 1)
    e = (kk - qq).astype(F32)
    e_ref[...] = e
    for hh in range(2):
        slope = slopes_ref[2 * pair + hh]
        esl_ref[0, hh] = e * slope
        esl_ref[1, hh] = e * (-slope)
    slots = ((s0_ref, mx0_ref), (s1_ref, mx1_ref))
    krefs = (k1_ref, k2_ref)

    def qk(kc, i, comp, hh):
        r0 = _aligned(kc * KC, KC)
        return jnp.dot(krefs[comp][pl.ds(r0, KC), :], qpad_ref[i, 2 * comp + hh], preferred_element_type=F32)

    def scores(c, i, half):
        s_ref, mx_ref = slots[half]
        kc = 2 * c + half
        off = lax.convert_element_type((kc - i) * KC, F32)
        dist = jnp.abs(e_ref[...] + off)
        for hh in range(2):
            bias_ref[half, hh] = dist * (-slopes_ref[2 * pair + hh])
        for hh in range(2):
            for comp in range(2):
                _stash_scores(qk(kc, i, comp, hh) + bias_ref[half, hh], s_ref, mx_ref, 2 * comp + hh)
                shift_ref[4 * half + 2 * comp + hh] = 0.0

    def scores_offdiag(c, i, half):
        s_ref, mx_ref = slots[half]
        kc = 2 * c + half
        off = lax.convert_element_type((kc - i) * KC, F32)
        after = jnp.where(kc > i, 1, 0)
        sign = jnp.where(kc > i, -1.0, 1.0)
        for hh in range(2):
            shift = sign * slopes_ref[2 * pair + hh] * off
            for comp in range(2):
                tile = 2 * comp + hh
                s = qk(kc, i, comp, hh) + esl_ref[after, hh]
                s_ref[tile] = s
                mx_ref[tile] = jnp.max(s.reshape(KC // 8, 8, tq), axis=0) + shift
                shift_ref[4 * half + tile] = shift

    def consume(c, i, half):
        s_ref, mx_ref = slots[half]
        for hh in range(2):
            vc1 = _with_ones(v_ref[2 * c + half, hh * D_DV:(hh + 1) * D_DV, :])
            for comp in range(2):
                tile = 2 * comp + hh
                _flash_update(s_ref, mx_ref, tile, vc1, m_ref, acc_ref, i, shift=shift_ref[4 * half + tile])

    _pipelined_sweep(nkc // 2, nq, scores, consume, scores_offdiag,
                     on_diagonal=lambda c, i, half: 2 * c + half == i, unroll=1)
    lam = (jnp.exp(jnp.sum(lq1_ref[...] * lk1_ref[...], axis=1, keepdims=True))
           - jnp.exp(jnp.sum(lq2_ref[...] * lk2_ref[...], axis=1, keepdims=True)) + lambda_init)

    def write_out(i, carry):
        outs = []
        for hh in range(2):
            o = _normalised(acc_ref[i, hh], D_DV) - lam * _normalised(acc_ref[i, 2 + hh], D_DV)
            ms = jnp.mean(o * o, axis=0, keepdims=True)
            outs.append(o * lax.rsqrt(ms + EPS) * sg_ref[...] * (1.0 - lambda_init))
        o_ref[pl.ds(pl.multiple_of(i * tq, tq), tq), :] = jnp.concatenate(outs, axis=0).T.astype(o_ref.dtype)
        return carry

    lax.fori_loop(0, nq, write_out, 0)


def _diff_attn(fm_d, main, lq1, lk1, lq2, lk2, subln_g, lambda_init, bsz, seq):
    t = bsz * seq
    tq = KC
    nq = seq // tq
    nkc = seq // KC
    pw = 2 * D_DH
    vw = 2 * D_DV
    dk0 = _MAIN_OFF["dk"] // pw
    vrow0 = (2 * D_HEADS * D_DH) // vw
    slopes = jnp.asarray([LOG2E * 2.0 ** (-8.0 * (h + 1) / D_HEADS) for h in range(D_HEADS)], F32)

    def qspec(c):
        return pl.BlockSpec((nq, pw, tq), lambda b, p: (b, 2 * c + p, 0))

    def kspec(c):
        return pl.BlockSpec((seq, pw), lambda b, p: (b, dk0 + 2 * c + p))

    vec = pl.BlockSpec((1, D_DH), lambda b, p: (0, 0))
    return pl.pallas_call(
        functools.partial(_diff_attn_kernel, nkc=nkc, tq=tq, lambda_init=lambda_init),
        out_shape=jax.ShapeDtypeStruct((t, D_WIDTH), BF16),
        grid=(bsz, D_HEADS // 2),
        in_specs=[pl.BlockSpec(memory_space=pltpu.SMEM),
                  qspec(0), qspec(1), kspec(0), kspec(1),
                  pl.BlockSpec((nkc, vw, KC), lambda b, p: (b, vrow0 + p, 0)),
                  vec, vec, vec, vec,
                  pl.BlockSpec((D_DV, 1), lambda b, p: (0, 0))],
        out_specs=pl.BlockSpec((seq, vw), lambda b, p: (b, p)),
        scratch_shapes=[pltpu.VMEM((nq, 4, 2 * D_DH, tq), BF16),
                        pltpu.VMEM((KC, tq), F32),
                        pltpu.VMEM((2, 2, KC, tq), F32),
                        pltpu.VMEM((2, 2, KC, tq), F32),
                        pltpu.SMEM((8,), F32),
                        pltpu.VMEM((4, KC, tq), F32), pltpu.VMEM((4, KC, tq), F32),
                        pltpu.VMEM((4, 8, tq), F32), pltpu.VMEM((4, 8, tq), F32),
                        pltpu.VMEM((nq, 4, 1, tq), F32),
                        pltpu.VMEM((nq, 4, D_DV + ONES_ROWS, tq), F32)],
        compiler_params=_cparams(("parallel", "arbitrary")),
        name="diff_attn",
    )(slopes, fm_d, fm_d, main, main, fm_d, lq1, lk1, lq2, lk2, subln_g)


def _merge_kernel(x_ref, gl_ref, ya_ref, yb_ref, yc_ref, yd_ref, wup_ref, wout_ref, o_ref):
    merged = None
    for g, y_ref in enumerate((ya_ref, yb_ref, yc_ref, yd_ref)):
        u = jnp.dot(y_ref[...], wup_ref[g], preferred_element_type=F32)
        gate = gl_ref[:, g * D_MODEL:(g + 1) * D_MODEL].astype(F32)
        term = _sigmoid(gate) * u
        merged = term if merged is None else merged + term
    o_ref[...] = x_ref[...] + jnp.dot(merged.astype(BF16), wout_ref[...], preferred_element_type=F32)


def _merge(x2, main, ya, yb, yc, yd, w_up, w_out):
    t = x2.shape[0]
    tm = min(512, t)
    ytile = pl.BlockSpec((tm, 512), lambda i: (i, 0))
    return pl.pallas_call(
        _merge_kernel,
        out_shape=jax.ShapeDtypeStruct((t, D_MODEL), F32),
        grid=(t // tm,),
        in_specs=[pl.BlockSpec((tm, D_MODEL), lambda i: (i, 0)),
                  pl.BlockSpec((tm, N_BRANCH * D_MODEL), lambda i: (i, 0)),
                  ytile, ytile, ytile, ytile,
                  pl.BlockSpec((N_BRANCH, 512, D_MODEL), lambda i: (0, 0, 0)),
                  pl.BlockSpec((D_MODEL, D_MODEL), lambda i: (0, 0))],
        out_specs=pl.BlockSpec((tm, D_MODEL), lambda i: (i, 0)),
        compiler_params=_cparams(("parallel",)),
        name="merge",
    )(x2, main, ya, yb, yc, yd, w_up, w_out)


def _ffn_kernel(x_ref, g_ref, wg_ref, wu_ref, wd_ref, fg_ref, o_ref, *, final):
    x = x_ref[...]
    ms = jnp.mean(x * x, axis=-1, keepdims=True)
    h = (x * lax.rsqrt(ms + EPS) * g_ref[...]).astype(BF16)
    acc = x
    for a, b in FF_SPLITS:
        gt = jnp.dot(h, wg_ref[:, a:b], preferred_element_type=F32)
        up = jnp.dot(h, wu_ref[:, a:b], preferred_element_type=F32)
        act = (gt * _sigmoid(gt)) * up
        acc = acc + jnp.dot(act.astype(BF16), wd_ref[a:b, :], preferred_element_type=F32)
    if final:
        ms2 = jnp.mean(acc * acc, axis=-1, keepdims=True)
        acc = acc * lax.rsqrt(ms2 + EPS) * fg_ref[...]
    o_ref[...] = acc


def _ffn(x2, g, wg, wu, wd, fg, final):
    t = x2.shape[0]
    tm = min(512, t)
    return pl.pallas_call(
        functools.partial(_ffn_kernel, final=final),
        out_shape=jax.ShapeDtypeStruct((t, D_MODEL), F32),
        grid=(t // tm,),
        in_specs=[pl.BlockSpec((tm, D_MODEL), lambda i: (i, 0)),
                  pl.BlockSpec((1, D_MODEL), lambda i: (0, 0)),
                  pl.BlockSpec((D_MODEL, D_FF), lambda i: (0, 0)),
                  pl.BlockSpec((D_MODEL, D_FF), lambda i: (0, 0)),
                  pl.BlockSpec((D_FF, D_MODEL), lambda i: (0, 0)),
                  pl.BlockSpec((1, D_MODEL), lambda i: (0, 0))],
        out_specs=pl.BlockSpec((tm, D_MODEL), lambda i: (i, 0)),
        compiler_params=_cparams(("parallel",)),
        name="ffn",
    )(x2, g, wg, wu, wd, fg)


def _rope_tables(seq):
    tpos = jnp.arange(seq)
    row = (tpos // GRID_W).astype(F32)
    colp = (tpos % GRID_W).astype(F32)
    n_freq = B_DH // 4
    inv = ROPE_THETA ** (-jnp.arange(n_freq, dtype=F32) / n_freq)
    ar = (row[:, None] * inv).T
    ac = (colp[:, None] * inv).T
    cos = jnp.concatenate([jnp.cos(ar), jnp.cos(ar), jnp.cos(ac), jnp.cos(ac)], axis=0)
    sin = jnp.concatenate([-jnp.sin(ar), jnp.sin(ar), -jnp.sin(ac), jnp.sin(ac)], axis=0)
    return cos.astype(F32), sin.astype(F32)


def _w_cols(w_in, name, scale=None):
    a, b = _OFF[name]
    w = w_in[..., a:b]
    return w if scale is None else w * scale


def _prepare_params(w_in, a_conv_w, a_gate_bias, c_rpb, w_up_a, w_up_b, w_up_c, w_up_d, w_out,
                    w_ffn_gate, w_ffn_up, w_ffn_down, seq):
    scales = {"cq": C_DH ** -0.5, "dq": D_DH ** -0.5 * LOG2E}
    transposed = lambda w: jnp.swapaxes(w, -1, -2).astype(BF16)
    p = {}
    p["w_main"] = jnp.concatenate([_w_cols(w_in, n, scales.get(n)) for n in _MAIN_ORDER], axis=-1).astype(BF16)
    ga, _ = _OFF["ag"]
    ngroups, ngates = A_HEADS // A_HPS, 4 * A_HPS
    gate_cols = [ty * A_HEADS + A_HPS * g + hh for g in range(ngroups) for ty in range(4) for hh in range(A_HPS)]
    wg = _w_cols(w_in, "ag")
    gb = a_gate_bias
    if gate_cols != list(range(4 * A_HEADS)):
        wg, gb = wg[..., jnp.asarray(gate_cols)], gb[..., jnp.asarray(gate_cols)]
    depth = w_in.shape[0]
    padded = jnp.pad(wg.reshape(depth, D_MODEL, ngroups, ngates), ((0, 0), (0, 0), (0, 0), (0, 128 - ngates)))
    p["w_gate"] = padded.reshape(depth, D_MODEL, ngroups * 128).astype(BF16)
    p["wt_gate"] = transposed(wg)
    gb = gb.reshape(depth, ngroups, ngates).astype(F32)
    p["gate_bias"] = jnp.pad(gb, ((0, 0), (0, 0), (0, 128 - ngates))).reshape(depth, 1, ngroups * 128)
    p["gate_bias_col"] = gb.reshape(depth, ngroups * ngates, 1)
    p["wt_av"] = transposed(_w_cols(w_in, "av"))
    p["wt_b"] = transposed(jnp.concatenate([_w_cols(w_in, n) for n in ("bq", "bk", "bv")], axis=-1))
    p["wt_d"] = transposed(jnp.concatenate([_w_cols(w_in, "dq", scales["dq"]), _w_cols(w_in, "dv")], axis=-1))
    cw = a_conv_w.astype(F32)
    gw_a = A_HPS * A_DH
    p["conv_w"] = jnp.stack([jnp.concatenate([cw[..., g * gw_a:(g + 1) * gw_a],
                                              cw[..., A_WIDTH + g * gw_a:A_WIDTH + (g + 1) * gw_a]], axis=-1)
                             for g in range(ngroups)], axis=1)
    p["natten_bias"] = jax.vmap(lambda r: _natten_bias_table(r, seq // GRID_W))(c_rpb)
    p["w_up"] = jnp.stack([w_up_a, w_up_b, w_up_c, w_up_d], axis=1).astype(BF16)
    p["w_out"] = w_out.astype(BF16)
    p["w_ffn_gate"] = w_ffn_gate.astype(BF16)
    p["w_ffn_up"] = w_ffn_up.astype(BF16)
    p["w_ffn_down"] = w_ffn_down.astype(BF16)
    return p


def _layer(x2, l, bsz, seq, cos, sin, p, norm1_g, a_norm_g, b_qnorm_g, b_knorm_g,
           d_lq1, d_lk1, d_lq2, d_lk2, d_subln_g, norm2_g, final_g, final):
    main, gates, h = _proj(x2, norm1_g.reshape(1, D_MODEL).astype(F32), p["w_main"], p["w_gate"])
    fm_d = _projt(h, p["wt_d"], BF16, "proj_fm_d")

    vt_a = _projt(h, p["wt_av"], BF16, "proj_fm_av", chunk=A_CHUNK)
    gates_t = _projt(h, p["wt_gate"], F32, "proj_gates_t", chunk=A_CHUNK)
    y_a = _mlstm(main, vt_a, gates, gates_t, p["gate_bias"], p["gate_bias_col"], p["conv_w"],
                 a_norm_g.reshape(1, A_WIDTH).astype(F32), bsz, seq)

    qt_b, k_b, v_b = _gqa_proj(h, p["wt_b"], cos, sin, b_qnorm_g.reshape(B_DH, 1).astype(F32),
                               b_knorm_g.reshape(B_DH, 1).astype(F32), seq)
    y_b = _gqa_attn(qt_b, k_b, v_b, bsz, seq)

    y_c = _natten(main, p["natten_bias"], bsz, seq)

    lambda_init = 0.8 - 0.6 * math.exp(-0.3 * l)
    vec = lambda a: a.reshape(1, D_DH).astype(F32)
    y_d = _diff_attn(fm_d, main, vec(d_lq1), vec(d_lk1), vec(d_lq2), vec(d_lk2),
                     d_subln_g.reshape(D_DV, 1).astype(F32), lambda_init, bsz, seq)

    x2 = _merge(x2, main, y_a, y_b, y_c, y_d, p["w_up"], p["w_out"])
    return _ffn(x2, norm2_g.reshape(1, D_MODEL).astype(F32), p["w_ffn_gate"], p["w_ffn_up"], p["w_ffn_down"],
                final_g.reshape(1, D_MODEL).astype(F32), final)


def kernel(x, norm1_g, w_in, a_conv_w, a_gate_bias, a_norm_g, b_qnorm_g, b_knorm_g, c_rpb, d_lambda_q1, d_lambda_k1, d_lambda_q2, d_lambda_k2, d_subln_g, w_up_a, w_up_b, w_up_c, w_up_d, w_out, norm2_g, w_ffn_gate, w_ffn_up, w_ffn_down, final_g):
    bsz, seq, _ = x.shape
    assert seq % KC == 0 and seq % GRID_W == 0
    cos, sin = _rope_tables(seq)
    params = _prepare_params(w_in, a_conv_w, a_gate_bias, c_rpb, w_up_a, w_up_b, w_up_c, w_up_d, w_out,
                             w_ffn_gate, w_ffn_up, w_ffn_down, seq)
    x2 = x.reshape(bsz * seq, D_MODEL)
    depth = norm1_g.shape[0]
    for l in range(depth):
        x2 = _layer(x2, l, bsz, seq, cos, sin, {k: v[l] for k, v in params.items()}, norm1_g[l], a_norm_g[l],
                    b_qnorm_g[l], b_knorm_g[l], d_lambda_q1[l], d_lambda_k1[l], d_lambda_q2[l],
                    d_lambda_k2[l], d_subln_g[l], norm2_g[l], final_g, l == depth - 1)
    return x2.reshape(bsz, seq, D_MODEL)
```

```python
import functools
import math

import jax
import jax.numpy as jnp
import numpy as np
from jax import lax
from jax.experimental import pallas as pl
from jax.experimental.pallas import tpu as pltpu

F32 = jnp.float32
BF16 = jnp.bfloat16

D_MODEL = 1024
DEPTH = 2
GRID_W = 64
EPS = 1e-6
N_BRANCH = 4
A_HEADS, A_DH, A_CHUNK = 4, 128, 128
A_WIDTH = A_HEADS * A_DH
B_HEADS, B_KV, B_DH = 8, 2, 64
B_WIDTH = B_HEADS * B_DH
ROPE_THETA = 10000.0
C_HEADS, C_DH, C_WIN_R, C_WIN_C = 8, 64, 8, 16
C_WIDTH = C_HEADS * C_DH
D_HEADS, D_DH = 4, 64
D_DV = 2 * D_DH
D_WIDTH = D_HEADS * D_DV
D_FF = ((8 * D_MODEL + 3 * 256 - 1) // (3 * 256)) * 256
FF_SPLITS = ((0, 1536), (1536, D_FF))

_SIZES = (A_WIDTH, A_WIDTH, A_WIDTH, A_WIDTH, 4 * A_HEADS, B_WIDTH, B_KV * B_DH, B_KV * B_DH,
          C_WIDTH, C_WIDTH, C_WIDTH, 2 * D_HEADS * D_DH, 2 * D_HEADS * D_DH, D_WIDTH, N_BRANCH * D_MODEL)
_NAMES = ("aq", "ak", "av", "ao", "ag", "bq", "bk", "bv", "cq", "ck", "cv", "dq", "dk", "dv", "gl")
_OFF = {}
_o = 0
for _n, _s in zip(_NAMES, _SIZES):
    _OFF[_n] = (_o, _o + _s)
    _o += _s

_MAIN_ORDER = ("gl", "aq", "ak", "ao", "cq", "ck", "cv", "dk")
_MAIN_OFF = {}
_o = 0
for _n in _MAIN_ORDER:
    _MAIN_OFF[_n] = _o
    _o += _OFF[_n][1] - _OFF[_n][0]
MAIN_N = _o

KC = 512
VMEM_LIMIT = 56 * 1024 * 1024


def _cparams(sem, vmem=VMEM_LIMIT):
    return pltpu.CompilerParams(dimension_semantics=sem, vmem_limit_bytes=vmem)


def _sigmoid(x):
    return 1.0 / (1.0 + jnp.exp(-x))


def _aligned(x, m):
    return x if isinstance(x, int) else pl.multiple_of(x, m)


def _proj_kernel(x_ref, g_ref, w_ref, wg_ref, o_ref, og_ref, h_ref):
    @pl.when(pl.program_id(1) == 0)
    def _():
        x = x_ref[...]
        ms = jnp.mean(x * x, axis=-1, keepdims=True)
        hb = (x * lax.rsqrt(ms + EPS) * g_ref[...]).astype(BF16)
        h_ref[...] = hb
        og_ref[...] = jnp.dot(hb, wg_ref[...], preferred_element_type=F32)

    o_ref[...] = jnp.dot(h_ref[...], w_ref[...], preferred_element_type=F32).astype(o_ref.dtype)


def _proj(x2, g, w_main, w_gate):
    t = x2.shape[0]
    tm = min(2048, t)
    n = w_main.shape[1]
    tn = 768 if n % 768 == 0 else 1024
    ng = w_gate.shape[1]
    return pl.pallas_call(
        _proj_kernel,
        out_shape=(jax.ShapeDtypeStruct((t, n), BF16),
                   jax.ShapeDtypeStruct((t, ng), F32),
                   jax.ShapeDtypeStruct((t, D_MODEL), BF16)),
        grid=(t // tm, n // tn),
        in_specs=[pl.BlockSpec((tm, D_MODEL), lambda i, j: (i, 0)),
                  pl.BlockSpec((1, D_MODEL), lambda i, j: (0, 0)),
                  pl.BlockSpec((D_MODEL, tn), lambda i, j: (0, j)),
                  pl.BlockSpec((D_MODEL, ng), lambda i, j: (0, 0))],
        out_specs=(pl.BlockSpec((tm, tn), lambda i, j: (i, j)),
                   pl.BlockSpec((tm, ng), lambda i, j: (i, 0)),
                   pl.BlockSpec((tm, D_MODEL), lambda i, j: (i, 0))),
        compiler_params=_cparams(("parallel", "arbitrary")),
        name="proj_main",
    )(x2, g, w_main, w_gate)


def _projt_kernel(h_ref, wt_ref, o_ref, *, nchunk, chunk):
    for c in range(nchunk):
        hc = h_ref[c * chunk:(c + 1) * chunk, :]
        o_ref[c] = lax.dot_general(wt_ref[...], hc, (((1,), (1,)), ((), ())),
                                   preferred_element_type=F32).astype(o_ref.dtype)


def _projt(h, wt, out_dtype, name, chunk=KC):
    t = h.shape[0]
    n = wt.shape[0]
    tm = min(2048, t)
    tn = n if n <= 768 else (768 if n % 768 == 0 else 512)
    nchunk = tm // chunk
    return pl.pallas_call(
        functools.partial(_projt_kernel, nchunk=nchunk, chunk=chunk),
        out_shape=jax.ShapeDtypeStruct((t // chunk, n, chunk), out_dtype),
        grid=(t // tm, n // tn),
        in_specs=[pl.BlockSpec((tm, D_MODEL), lambda i, j: (i, 0)),
                  pl.BlockSpec((tn, D_MODEL), lambda i, j: (j, 0))],
        out_specs=pl.BlockSpec((nchunk, tn, chunk), lambda i, j: (i, j, 0)),
        compiler_params=_cparams(("parallel", "arbitrary")),
        name=name,
    )(h, wt)


def _log_sigmoid(x):
    return jnp.minimum(x, 0.0) - jnp.log1p(jnp.exp(-jnp.abs(x)))


def _tri_dot(mat, x):
    hi = x.astype(BF16)
    r1 = x - hi.astype(F32)
    mid = r1.astype(BF16)
    lo = (r1 - mid.astype(F32)).astype(BF16)
    return (jnp.dot(mat, hi, preferred_element_type=F32)
            + jnp.dot(mat, mid, preferred_element_type=F32)
            + jnp.dot(mat, lo, preferred_element_type=F32))


def _split3(x):
    hi = x.astype(BF16)
    r1 = x - hi.astype(F32)
    mid = r1.astype(BF16)
    return hi, mid, (r1 - mid.astype(F32)).astype(BF16)


def _tri_dot_r(x, mat):
    return sum(jnp.dot(piece, mat, preferred_element_type=F32) for piece in _split3(x))


NROWS = 8
A_HPS = 4


def _mlstm_kernel(q_ref, k_ref, vt_ref, o_ref, g_ref, gt_ref, gb_ref, gbc_ref, cw_ref, ng_ref, y_ref,
                  qs_ref, ks_ref, hs_ref, st_ref, m_ref, *, seq):
    L = A_CHUNK
    nc = seq // L
    hw = A_HPS * A_DH

    rowi = lax.broadcasted_iota(jnp.int32, (L, hw), 0)

    def conv_body(c, carry):
        r0 = pl.multiple_of(c * L, L)
        pstart = pl.multiple_of(jnp.maximum(r0 - 16, 0), 16)
        nstart = pl.multiple_of(jnp.minimum(r0 + L, seq - 16), 16)
        has_prev = jnp.where(c > 0, 1.0, 0.0)
        has_next = jnp.where(c < nc - 1, 1.0, 0.0)
        for src, dst, woff, scale in ((q_ref, qs_ref, 0, 1.0), (k_ref, ks_ref, hw, A_DH ** -0.5)):
            xc = src[pl.ds(r0, L), :].astype(F32)
            prev = src[pl.ds(pstart, 16), :].astype(F32)[15:16] * has_prev
            nxt = src[pl.ds(nstart, 16), :].astype(F32)[0:1] * has_next
            xp = jnp.where(rowi == 0, prev, pltpu.roll(xc, 1, 0))
            xn = jnp.where(rowi == L - 1, nxt, pltpu.roll(xc, L - 1, 0))
            w = cw_ref[:, woff:woff + hw]
            y = xp * w[0:1] + xc * w[1:2] + xn * w[2:3]
            y = y * _sigmoid(y)
            dst[pl.ds(r0, L), :] = (y * scale).astype(BF16)
        return carry

    lax.fori_loop(0, nc, conv_body, 0)

    st_ref[...] = jnp.zeros(st_ref.shape, F32)
    m_ref[...] = jnp.zeros(m_ref.shape, F32)

    ti = lax.broadcasted_iota(jnp.int32, (L, L), 0)
    tj = lax.broadcasted_iota(jnp.int32, (L, L), 1)
    lower = tj <= ti
    upper = tj >= ti
    lmat = jnp.where(lower, 1.0, 0.0).astype(BF16)
    umat = jnp.where(upper, 1.0, 0.0).astype(BF16)
    nt = (((1,), (1,)), ((), ()))

    def step(cf, cb):
        chains = []
        for d, c in ((0, cf), (1, cb)):
            r0 = pl.multiple_of(c * L, L)
            gcol = g_ref[pl.ds(r0, L), :] + gb_ref[...]
            bcol_all = _tri_dot(lmat if d == 0 else umat, _log_sigmoid(gcol))
            grow = gt_ref[c] + gbc_ref[...]
            brow_all = _tri_dot_r(_log_sigmoid(grow), umat if d == 0 else lmat)
            for hh in range(A_HPS):
                il = 2 * A_HPS * d + hh
                fl = il + A_HPS
                ch = dict(d=d, idx=A_HPS * d + hh, c=c)
                ch["acol"] = gcol[:, il:il + 1] - bcol_all[:, fl:fl + 1]
                ch["irow"] = grow[il:il + 1, :]
                ch["brow"] = brow_all[fl:fl + 1, :]
                ch["q"] = qs_ref[pl.ds(r0, L), hh * A_DH:(hh + 1) * A_DH]
                ch["k"] = ks_ref[pl.ds(r0, L), hh * A_DH:(hh + 1) * A_DH]
                ch["vt"] = vt_ref[c, hh * A_DH:(hh + 1) * A_DH, :]
                ch["state"] = st_ref[ch["idx"]]
                ch["st"] = lax.dot_general(ch["k"], ch["q"], nt, preferred_element_type=F32)
                ch["it"] = lax.dot_general(ch["state"].astype(BF16), ch["q"], nt,
                                           preferred_element_type=F32)
                chains.append(ch)
        for ch in chains:
            d, brow = ch["d"], ch["brow"]
            mask = upper if d == 0 else lower
            m_prev = m_ref[ch["idx"]][:, 0:1]
            dmt = jnp.where(mask, brow + ch["acol"], -jnp.inf)
            inter = brow + m_prev
            mt = jnp.maximum(inter, jnp.max(dmt, axis=0, keepdims=True))
            ch["w_inter"] = jnp.exp(inter - mt)
            ch["floor"] = jnp.exp(-mt)
            sqk = ch["st"] * jnp.exp(dmt - mt)
            ch["sqk_sum"] = jnp.sum(sqk, axis=0, keepdims=True)
            ch["sqk"] = sqk.astype(BF16)
            bl = brow[:, L - 1:L] if d == 0 else brow[:, 0:1]
            gvec = bl - brow + ch["irow"]
            m_new = jnp.maximum(bl + m_prev, jnp.max(gvec, axis=1, keepdims=True))
            ch["wc"] = jnp.exp(bl + m_prev - m_new)
            ws = jnp.exp(gvec - m_new)
            ch["lhs"] = jnp.concatenate([ch["vt"].astype(F32) * ws, jnp.broadcast_to(ws, (NROWS, L))],
                                        axis=0).astype(BF16)
            m_ref[ch["idx"]] = jnp.broadcast_to(m_new, (1, 128))
        for ch in chains:
            ch["pv"] = jnp.dot(ch["vt"], ch["sqk"], preferred_element_type=F32)
            ch["upd"] = jnp.dot(ch["lhs"], ch["k"], preferred_element_type=F32)
        outs = {0: [], 1: []}
        for ch in chains:
            num = ch["w_inter"] * ch["it"][:A_DH] + ch["pv"]
            den = ch["w_inter"] * ch["it"][A_DH:A_DH + 1] + ch["sqk_sum"]
            outs[ch["d"]].append(num / jnp.maximum(jnp.abs(den), ch["floor"]))
            st_ref[ch["idx"]] = ch["wc"] * ch["state"] + ch["upd"]
        return outs

    def finalize(tots, r0):
        parts = []
        for tot in tots:
            mu = jnp.mean(tot, axis=0, keepdims=True)
            cen = tot - mu
            var = jnp.mean(cen * cen, axis=0, keepdims=True)
            parts.append((cen * lax.rsqrt(var + EPS)).T)
        hn = jnp.concatenate(parts, axis=1) * ng_ref[...]
        y_ref[pl.ds(r0, L), :] = (hn * _sigmoid(o_ref[pl.ds(r0, L), :].astype(F32))).astype(y_ref.dtype)

    def first_half(j, carry):
        outs = step(j, nc - 1 - j)
        for d, c in ((0, j), (1, nc - 1 - j)):
            for hh in range(A_HPS):
                hs_ref[c, hh] = outs[d][hh]
        return carry

    def second_half(j, carry):
        outs = step(j, nc - 1 - j)
        for d, c in ((0, j), (1, nc - 1 - j)):
            finalize([hs_ref[c, hh] + outs[d][hh] for hh in range(A_HPS)], pl.multiple_of(c * L, L))
        return carry

    lax.fori_loop(0, nc // 2, first_half, 0)
    lax.fori_loop(nc // 2, nc, second_half, 0)


def _mlstm(main, vt, gates, gates_t, gate_bias, gate_bias_col, conv_w, norm_g, bsz, seq):
    t = bsz * seq
    hw = A_HPS * A_DH
    nc = seq // A_CHUNK
    ngates = 4 * A_HPS
    assert nc % 2 == 0
    once = pl.Buffered(1)

    def col(name):
        base = _MAIN_OFF[name] // hw
        return pl.BlockSpec((seq, hw), lambda b, p: (b, base + p), pipeline_mode=once)

    return pl.pallas_call(
        functools.partial(_mlstm_kernel, seq=seq),
        out_shape=jax.ShapeDtypeStruct((t, A_WIDTH), BF16),
        grid=(bsz, A_HEADS // A_HPS),
        in_specs=[col("aq"), col("ak"),
                  pl.BlockSpec((nc, hw, A_CHUNK), lambda b, p: (b, p, 0), pipeline_mode=once),
                  col("ao"),
                  pl.BlockSpec((seq, 128), lambda b, p: (b, p), pipeline_mode=once),
                  pl.BlockSpec((nc, ngates, A_CHUNK), lambda b, p: (b, p, 0), pipeline_mode=once),
                  pl.BlockSpec((1, 128), lambda b, p: (0, p)),
                  pl.BlockSpec((ngates, 1), lambda b, p: (p, 0)),
                  pl.BlockSpec((None, 3, 2 * hw), lambda b, p: (p, 0, 0)),
                  pl.BlockSpec((1, hw), lambda b, p: (0, p))],
        out_specs=pl.BlockSpec((seq, hw), lambda b, p: (b, p)),
        scratch_shapes=[pltpu.VMEM((seq, hw), BF16), pltpu.VMEM((seq, hw), BF16),
                        pltpu.VMEM((nc, A_HPS, A_DH, A_CHUNK), F32),
                        pltpu.VMEM((2 * A_HPS, A_DH + NROWS, A_DH), F32),
                        pltpu.VMEM((2 * A_HPS, 1, 128), F32)],
        compiler_params=_cparams(("parallel", "arbitrary")),
        name="mlstm",
    )(main, main, vt, main, gates, gates_t, gate_bias, gate_bias_col, conv_w, norm_g)


LOG2E = 1.4426950408889634


def _stash_scores(s, s_ref, mx_ref, idx):
    tk, tq = s.shape
    s_ref[idx] = s
    mx_ref[idx] = jnp.max(s.reshape(tk // 8, 8, tq), axis=0)


ONES_ROWS = 8


def _with_ones(vc):
    return jnp.concatenate([vc, jnp.ones((ONES_ROWS, vc.shape[1]), vc.dtype)], axis=0)


def _flash_update(s_ref, mx_ref, tile, vc1, m_ref, acc_ref, qb, shift=None):
    m_old = m_ref[qb, tile]
    m_new = jnp.maximum(m_old, jnp.max(mx_ref[tile], axis=0, keepdims=True))
    alpha = jnp.exp2(m_old - m_new)
    p = jnp.exp2(s_ref[tile] - (m_new if shift is None else m_new - shift))
    acc_ref[qb, tile] = alpha * acc_ref[qb, tile] + jnp.dot(vc1, p.astype(BF16), preferred_element_type=F32)
    m_ref[qb, tile] = m_new


def _normalised(acc, dv):
    return acc[:dv] / acc[dv:dv + 1]


HK = KC // 2


def _pipelined_sweep(nkc, nq, scores, consume, unroll):
    total = nkc * nq

    def step(it):
        c, i = it // nq, it % nq
        nxt = it + 1
        scores(c, i, 1)
        consume(c, i, 0)
        scores(nxt // nq, nxt % nq, 0)
        consume(c, i, 1)

    scores(0, 0, 0)
    trips = (total - 1) // unroll

    def body(j, carry):
        for u in range(unroll):
            step(j * unroll + u)
        return carry

    lax.fori_loop(0, trips, body, 0)
    for it in range(trips * unroll, total - 1):
        step(it)
    scores(nkc - 1, nq - 1, 1)
    consume(nkc - 1, nq - 1, 0)
    consume(nkc - 1, nq - 1, 1)


def _gqa_proj_kernel(h_ref, wt_ref, cos_ref, sin_ref, qg_ref, kg_ref, q_ref, k_ref, v_ref, *, nchunk):
    def norm_rope(xh, g, cos, sin):
        ms = jnp.mean(xh * xh, axis=0, keepdims=True)
        xn = xh * lax.rsqrt(ms + EPS) * g
        partner = jnp.concatenate([xn[16:32], xn[0:16], xn[48:64], xn[32:48]], axis=0)
        return xn * cos + partner * sin

    for c in range(nchunk):
        tok = slice(c * KC, (c + 1) * KC)
        x = lax.dot_general(wt_ref[...], h_ref[tok, :], (((1,), (1,)), ((), ())), preferred_element_type=F32)
        cos, sin = cos_ref[:, tok], sin_ref[:, tok]
        for h in range(B_HEADS):
            qh = norm_rope(x[h * B_DH:(h + 1) * B_DH], qg_ref[...], cos, sin) * (B_DH ** -0.5 * LOG2E)
            q_ref[c, h * B_DH:(h + 1) * B_DH, :] = qh.astype(BF16)
        ks = [norm_rope(x[B_WIDTH + g * B_DH:B_WIDTH + (g + 1) * B_DH], kg_ref[...], cos, sin)
              for g in range(B_KV)]
        k_ref[tok, :] = jnp.concatenate(ks, axis=0).T.astype(BF16)
        v_ref[c] = x[B_WIDTH + B_KV * B_DH:].astype(BF16)


def _gqa_proj(h, wt_b, cos, sin, qg, kg, seq):
    t = h.shape[0]
    tm = min(2048, seq)
    nchunk = tm // KC
    spt = seq // tm
    kvw = B_KV * B_DH
    nrows = B_WIDTH + 2 * kvw
    return pl.pallas_call(
        functools.partial(_gqa_proj_kernel, nchunk=nchunk),
        out_shape=(jax.ShapeDtypeStruct((t // KC, B_WIDTH, KC), BF16),
                   jax.ShapeDtypeStruct((t, kvw), BF16),
                   jax.ShapeDtypeStruct((t // KC, kvw, KC), BF16)),
        grid=(t // tm,),
        in_specs=[pl.BlockSpec((tm, D_MODEL), lambda i: (i, 0)),
                  pl.BlockSpec((nrows, D_MODEL), lambda i: (0, 0)),
                  pl.BlockSpec((B_DH, tm), lambda i: (0, i % spt)),
                  pl.BlockSpec((B_DH, tm), lambda i: (0, i % spt)),
                  pl.BlockSpec((B_DH, 1), lambda i: (0, 0)),
                  pl.BlockSpec((B_DH, 1), lambda i: (0, 0))],
        out_specs=(pl.BlockSpec((nchunk, B_WIDTH, KC), lambda i: (i, 0, 0)),
                   pl.BlockSpec((tm, kvw), lambda i: (i, 0)),
                   pl.BlockSpec((nchunk, kvw, KC), lambda i: (i, 0, 0))),
        compiler_params=_cparams(("parallel",)),
        name="gqa_proj",
    )(h, wt_b, cos, sin, qg, kg)


def _gqa_attn_kernel(q_ref, k_ref, v_ref, o_ref, qpad_ref, s0_ref, s1_ref, mx0_ref, mx1_ref,
                     m_ref, acc_ref, *, nkc, tq):
    g = pl.program_id(1)
    rep = B_HEADS // B_KV
    nq = q_ref.shape[0]
    row = lax.broadcasted_iota(jnp.int32, (2 * B_DH, tq), 0)
    sel = (row // B_DH) == g

    def pad_queries(i, carry):
        for r in range(rep):
            qh = q_ref[i, r * B_DH:(r + 1) * B_DH, :].astype(F32)
            qpad_ref[i, r] = jnp.where(sel, jnp.concatenate([qh, qh], axis=0), 0.0).astype(BF16)
        return carry

    lax.fori_loop(0, nq, pad_queries, 0)
    m_ref[...] = jnp.full(m_ref.shape, -jnp.inf, F32)
    acc_ref[...] = jnp.zeros(acc_ref.shape, F32)
    slots = ((s0_ref, mx0_ref), (s1_ref, mx1_ref))

    def scores(c, i, half):
        s_ref, mx_ref = slots[half]
        kc = k_ref[pl.ds(_aligned(c * KC + half * HK, HK), HK), :]
        for r in range(rep):
            _stash_scores(jnp.dot(kc, qpad_ref[i, r], preferred_element_type=F32), s_ref, mx_ref, r)

    def consume(c, i, half):
        s_ref, mx_ref = slots[half]
        vc1 = _with_ones(v_ref[c, :, half * HK:(half + 1) * HK])
        for r in range(rep):
            _flash_update(s_ref, mx_ref, r, vc1, m_ref, acc_ref, i)

    _pipelined_sweep(nkc, nq, scores, consume, unroll=4)

    def write_out(i, carry):
        outs = [_normalised(acc_ref[i, r], B_DH) for r in range(rep)]
        o_ref[pl.ds(pl.multiple_of(i * tq, tq), tq), :] = jnp.concatenate(outs, axis=0).T.astype(o_ref.dtype)
        return carry

    lax.fori_loop(0, nq, write_out, 0)


def _gqa_attn(qt, k_tok, v_fm, bsz, seq):
    t = bsz * seq
    tq = KC
    nq = seq // tq
    nkc = seq // KC
    rep = B_HEADS // B_KV
    gw = rep * B_DH
    return pl.pallas_call(
        functools.partial(_gqa_attn_kernel, nkc=nkc, tq=tq),
        out_shape=jax.ShapeDtypeStruct((t, B_WIDTH), BF16),
        grid=(bsz, B_KV),
        in_specs=[pl.BlockSpec((nq, gw, tq), lambda b, g: (b, g, 0)),
                  pl.BlockSpec((seq, B_KV * B_DH), lambda b, g: (b, 0)),
                  pl.BlockSpec((nkc, B_DH, KC), lambda b, g: (b, g, 0))],
        out_specs=pl.BlockSpec((seq, gw), lambda b, g: (b, g)),
        scratch_shapes=[pltpu.VMEM((nq, rep, 2 * B_DH, tq), BF16),
                        pltpu.VMEM((rep, HK, tq), F32), pltpu.VMEM((rep, HK, tq), F32),
                        pltpu.VMEM((rep, 8, tq), F32), pltpu.VMEM((rep, 8, tq), F32),
                        pltpu.VMEM((nq, rep, 1, tq), F32),
                        pltpu.VMEM((nq, rep, B_DH + ONES_ROWS, tq), F32)],
        compiler_params=_cparams(("parallel", "arbitrary")),
        name="gqa_attn",
    )(qt, k_tok, v_fm)


NAT_ROWS = 8


def _natten_kernel(q_ref, k_ref, v_ref, *rest, rows):
    bias_refs, o_ref = rest[:NAT_ROWS], rest[NAT_ROWS]
    nkeys = C_WIN_R * GRID_W
    npair = C_HEADS // 2
    own = (lax.broadcasted_iota(jnp.int32, (2 * GRID_W, 2 * C_DH), 0) // GRID_W
           == lax.broadcasted_iota(jnp.int32, (2 * GRID_W, 2 * C_DH), 1) // C_DH)
    starts = []
    for rr in range(NAT_ROWS):
        r = pl.program_id(1) * NAT_ROWS + rr
        rs = jnp.clip(r - C_WIN_R // 2, 0, rows - C_WIN_R)
        starts.append(pl.multiple_of(rs * GRID_W, GRID_W))
    units = [(rr, p) for rr in range(NAT_ROWS) for p in range(npair)]
    scores = []
    for rr, p in units:
        kp = k_ref[pl.ds(starts[rr], nkeys), p * 128:(p + 1) * 128]
        qp = q_ref[rr * GRID_W:(rr + 1) * GRID_W, p * 128:(p + 1) * 128].astype(F32)
        qm = jnp.where(own, jnp.concatenate([qp, qp], axis=0), 0.0).astype(BF16)
        s = lax.dot_general(qm, kp, (((1,), (1,)), ((), ())), preferred_element_type=F32)
        scores.append(s + bias_refs[rr][0, 2 * p:2 * p + 2].reshape(2 * GRID_W, nkeys))
    probs = []
    for s in scores:
        e = jnp.exp(s - jnp.max(s, axis=1, keepdims=True))
        probs.append((e.astype(BF16), jnp.sum(e, axis=1, keepdims=True)))
    outs = [[] for _ in range(NAT_ROWS)]
    for (rr, p), (e, l) in zip(units, probs):
        vp = v_ref[pl.ds(starts[rr], nkeys), p * 128:(p + 1) * 128]
        o = jnp.where(own, jnp.dot(e, vp, preferred_element_type=F32) / l, 0.0)
        outs[rr].append(o[:GRID_W] + o[GRID_W:])
    o_ref[...] = jnp.concatenate([jnp.concatenate(o, axis=1) for o in outs], axis=0).astype(o_ref.dtype)


def _natten_bias_table(rpb, rows):
    c = np.arange(GRID_W)[:, None]
    kc = np.arange(GRID_W)[None, :]
    cs = np.clip(c - C_WIN_C // 2, 0, GRID_W - C_WIN_C)
    valid = (kc >= cs) & (kc < cs + C_WIN_C)
    nd = 2 * C_WIN_C - 1
    onehot = (kc - c + (C_WIN_C - 1))[None] == np.arange(nd)[:, None, None]
    toe = jnp.sum(jnp.where(onehot[None, None], rpb.astype(F32)[:, :, :, None, None], 0.0), axis=2)
    toe = jnp.where(valid[None, None], toe, -1e30)
    tbl = jnp.stack([toe[:, C_WIN_R - 1 - v:2 * C_WIN_R - 1 - v] for v in range(C_WIN_R)], axis=0)
    return jnp.swapaxes(tbl, 2, 3).reshape(C_WIN_R, C_HEADS, GRID_W, C_WIN_R * GRID_W)


def _natten(main, bias_tbl, bsz, seq):
    t = bsz * seq
    rows = seq // GRID_W
    assert rows >= C_WIN_R
    half = C_WIN_R // 2

    def col(name):
        return _MAIN_OFF[name] // C_WIDTH

    cq, ck, cv = col("cq"), col("ck"), col("cv")
    assert rows % NAT_ROWS == 0
    steps = rows // NAT_ROWS

    def bias_spec(rr):
        def variant(b, i):
            r = i * NAT_ROWS + rr
            return (r - jnp.clip(r - half, 0, rows - C_WIN_R), 0, 0, 0)
        return pl.BlockSpec((1, C_HEADS, GRID_W, C_WIN_R * GRID_W), variant)

    return pl.pallas_call(
        functools.partial(_natten_kernel, rows=rows),
        out_shape=jax.ShapeDtypeStruct((t, C_WIDTH), BF16),
        grid=(bsz, steps),
        in_specs=[pl.BlockSpec((NAT_ROWS * GRID_W, C_WIDTH), lambda b, i: (b * steps + i, cq)),
                  pl.BlockSpec((seq, C_WIDTH), lambda b, i: (b, ck)),
                  pl.BlockSpec((seq, C_WIDTH), lambda b, i: (b, cv))]
                 + [bias_spec(rr) for rr in range(NAT_ROWS)],
        out_specs=pl.BlockSpec((NAT_ROWS * GRID_W, C_WIDTH), lambda b, i: (b * steps + i, 0)),
        compiler_params=_cparams(("parallel", "arbitrary")),
        name="natten",
    )(main, main, main, *([bias_tbl] * NAT_ROWS))


def _diff_attn_kernel(slopes_ref, q1_ref, q2_ref, k1_ref, k2_ref, v_ref, lq1_ref, lk1_ref, lq2_ref, lk2_ref,
                      sg_ref, o_ref, qpad_ref, esl_ref, shift_ref, s0_ref, s1_ref, mx0_ref, mx1_ref,
                      m_ref, acc_ref, *, nkc, tq, lambda_init):
    pair = pl.program_id(1)
    nq = q1_ref.shape[0]
    row = lax.broadcasted_iota(jnp.int32, (2 * D_DH, tq), 0)

    def pad_queries(i, carry):
        for c, qr in enumerate((q1_ref, q2_ref)):
            qf = qr[i].astype(F32)
            for hh in range(2):
                qpad_ref[i, 2 * c + hh] = jnp.where((row // D_DH) == hh, qf, 0.0).astype(BF16)
        return carry

    lax.fori_loop(0, nq, pad_queries, 0)
    m_ref[...] = jnp.full(m_ref.shape, -jnp.inf, F32)
    acc_ref[...] = jnp.zeros(acc_ref.shape, F32)
    kk = lax.broadcasted_iota(jnp.int32, (KC, tq), 0)
    qq = lax.broadcasted_iota(jnp.int32, (KC, tq), 1)
    e = (kk - qq).astype(F32)
    for hh in range(2):
        slope = slopes_ref[2 * pair + hh]
        esl_ref[0, hh] = e * slope
        esl_ref[1, hh] = e * (-slope)
        esl_ref[2, hh] = jnp.abs(e) * (-slope)
    slots = ((s0_ref, mx0_ref), (s1_ref, mx1_ref))
    krefs = (k1_ref, k2_ref)

    def scores(c, i, half):
        s_ref, mx_ref = slots[half]
        kc = 2 * c + half
        r0 = _aligned(kc * KC, KC)
        off = lax.convert_element_type((kc - i) * KC, F32)
        side = jnp.where(kc > i, 1, jnp.where(kc == i, 2, 0))
        sign = jnp.where(kc > i, -1.0, 1.0)
        for hh in range(2):
            shift = sign * slopes_ref[2 * pair + hh] * off
            for comp in range(2):
                tile = 2 * comp + hh
                s = jnp.dot(krefs[comp][pl.ds(r0, KC), :], qpad_ref[i, tile], preferred_element_type=F32)
                s = s + esl_ref[side, hh]
                s_ref[tile] = s
                mx_ref[tile] = jnp.max(s.reshape(KC // 8, 8, tq), axis=0) + shift
                shift_ref[4 * half + tile] = shift

    def consume(c, i, half):
        s_ref, mx_ref = slots[half]
        for hh in range(2):
            vc1 = _with_ones(v_ref[2 * c + half, hh * D_DV:(hh + 1) * D_DV, :])
            for comp in range(2):
                tile = 2 * comp + hh
                _flash_update(s_ref, mx_ref, tile, vc1, m_ref, acc_ref, i, shift=shift_ref[4 * half + tile])

    _pipelined_sweep(nkc // 2, nq, scores, consume, unroll=2)
    lam = (jnp.exp(jnp.sum(lq1_ref[...] * lk1_ref[...], axis=1, keepdims=True))
           - jnp.exp(jnp.sum(lq2_ref[...] * lk2_ref[...], axis=1, keepdims=True)) + lambda_init)

    def write_out(i, carry):
        outs = []
        for hh in range(2):
            o = _normalised(acc_ref[i, hh], D_DV) - lam * _normalised(acc_ref[i, 2 + hh], D_DV)
            ms = jnp.mean(o * o, axis=0, keepdims=True)
            outs.append(o * lax.rsqrt(ms + EPS) * sg_ref[...] * (1.0 - lambda_init))
        o_ref[pl.ds(pl.multiple_of(i * tq, tq), tq), :] = jnp.concatenate(outs, axis=0).T.astype(o_ref.dtype)
        return carry

    lax.fori_loop(0, nq, write_out, 0)


def _diff_attn(fm_d, main, lq1, lk1, lq2, lk2, subln_g, lambda_init, bsz, seq):
    t = bsz * seq
    tq = KC
    nq = seq // tq
    nkc = seq // KC
    pw = 2 * D_DH
    vw = 2 * D_DV
    dk0 = _MAIN_OFF["dk"] // pw
    vrow0 = (2 * D_HEADS * D_DH) // vw
    slopes = jnp.asarray([LOG2E * 2.0 ** (-8.0 * (h + 1) / D_HEADS) for h in range(D_HEADS)], F32)

    def qspec(c):
        return pl.BlockSpec((nq, pw, tq), lambda b, p: (b, 2 * c + p, 0))

    def kspec(c):
        return pl.BlockSpec((seq, pw), lambda b, p: (b, dk0 + 2 * c + p))

    vec = pl.BlockSpec((1, D_DH), lambda b, p: (0, 0))
    return pl.pallas_call(
        functools.partial(_diff_attn_kernel, nkc=nkc, tq=tq, lambda_init=lambda_init),
        out_shape=jax.ShapeDtypeStruct((t, D_WIDTH), BF16),
        grid=(bsz, D_HEADS // 2),
        in_specs=[pl.BlockSpec(memory_space=pltpu.SMEM),
                  qspec(0), qspec(1), kspec(0), kspec(1),
                  pl.BlockSpec((nkc, vw, KC), lambda b, p: (b, vrow0 + p, 0)),
                  vec, vec, vec, vec,
                  pl.BlockSpec((D_DV, 1), lambda b, p: (0, 0))],
        out_specs=pl.BlockSpec((seq, vw), lambda b, p: (b, p)),
        scratch_shapes=[pltpu.VMEM((nq, 4, 2 * D_DH, tq), BF16),
                        pltpu.VMEM((3, 2, KC, tq), F32),
                        pltpu.SMEM((8,), F32),
                        pltpu.VMEM((4, KC, tq), F32), pltpu.VMEM((4, KC, tq), F32),
                        pltpu.VMEM((4, 8, tq), F32), pltpu.VMEM((4, 8, tq), F32),
                        pltpu.VMEM((nq, 4, 1, tq), F32),
                        pltpu.VMEM((nq, 4, D_DV + ONES_ROWS, tq), F32)],
        compiler_params=_cparams(("parallel", "arbitrary")),
        name="diff_attn",
    )(slopes, fm_d, fm_d, main, main, fm_d, lq1, lk1, lq2, lk2, subln_g)


def _merge_kernel(x_ref, gl_ref, ya_ref, yb_ref, yc_ref, yd_ref, wup_ref, wout_ref, o_ref):
    merged = None
    for g, y_ref in enumerate((ya_ref, yb_ref, yc_ref, yd_ref)):
        u = jnp.dot(y_ref[...], wup_ref[g], preferred_element_type=F32)
        gate = gl_ref[:, g * D_MODEL:(g + 1) * D_MODEL].astype(F32)
        term = _sigmoid(gate) * u
        merged = term if merged is None else merged + term
    o_ref[...] = x_ref[...] + jnp.dot(merged.astype(BF16), wout_ref[...], preferred_element_type=F32)


def _merge(x2, main, ya, yb, yc, yd, w_up, w_out):
    t = x2.shape[0]
    tm = min(512, t)
    ytile = pl.BlockSpec((tm, 512), lambda i: (i, 0))
    return pl.pallas_call(
        _merge_kernel,
        out_shape=jax.ShapeDtypeStruct((t, D_MODEL), F32),
        grid=(t // tm,),
        in_specs=[pl.BlockSpec((tm, D_MODEL), lambda i: (i, 0)),
                  pl.BlockSpec((tm, N_BRANCH * D_MODEL), lambda i: (i, 0)),
                  ytile, ytile, ytile, ytile,
                  pl.BlockSpec((N_BRANCH, 512, D_MODEL), lambda i: (0, 0, 0)),
                  pl.BlockSpec((D_MODEL, D_MODEL), lambda i: (0, 0))],
        out_specs=pl.BlockSpec((tm, D_MODEL), lambda i: (i, 0)),
        compiler_params=_cparams(("parallel",)),
        name="merge",
    )(x2, main, ya, yb, yc, yd, w_up, w_out)


def _ffn_kernel(x_ref, g_ref, wg_ref, wu_ref, wd_ref, fg_ref, o_ref, *, final):
    x = x_ref[...]
    ms = jnp.mean(x * x, axis=-1, keepdims=True)
    h = (x * lax.rsqrt(ms + EPS) * g_ref[...]).astype(BF16)
    acc = x
    for a, b in FF_SPLITS:
        gt = jnp.dot(h, wg_ref[:, a:b], preferred_element_type=F32)
        up = jnp.dot(h, wu_ref[:, a:b], preferred_element_type=F32)
        act = (gt * _sigmoid(gt)) * up
        acc = acc + jnp.dot(act.astype(BF16), wd_ref[a:b, :], preferred_element_type=F32)
    if final:
        ms2 = jnp.mean(acc * acc, axis=-1, keepdims=True)
        acc = acc * lax.rsqrt(ms2 + EPS) * fg_ref[...]
    o_ref[...] = acc


def _ffn(x2, g, wg, wu, wd, fg, final):
    t = x2.shape[0]
    tm = min(512, t)
    return pl.pallas_call(
        functools.partial(_ffn_kernel, final=final),
        out_shape=jax.ShapeDtypeStruct((t, D_MODEL), F32),
        grid=(t // tm,),
        in_specs=[pl.BlockSpec((tm, D_MODEL), lambda i: (i, 0)),
                  pl.BlockSpec((1, D_MODEL), lambda i: (0, 0)),
                  pl.BlockSpec((D_MODEL, D_FF), lambda i: (0, 0)),
                  pl.BlockSpec((D_MODEL, D_FF), lambda i: (0, 0)),
                  pl.BlockSpec((D_FF, D_MODEL), lambda i: (0, 0)),
                  pl.BlockSpec((1, D_MODEL), lambda i: (0, 0))],
        out_specs=pl.BlockSpec((tm, D_MODEL), lambda i: (i, 0)),
        compiler_params=_cparams(("parallel",)),
        name="ffn",
    )(x2, g, wg, wu, wd, fg)


def _rope_tables(seq):
    tpos = jnp.arange(seq)
    row = (tpos // GRID_W).astype(F32)
    colp = (tpos % GRID_W).astype(F32)
    n_freq = B_DH // 4
    inv = ROPE_THETA ** (-jnp.arange(n_freq, dtype=F32) / n_freq)
    ar = (row[:, None] * inv).T
    ac = (colp[:, None] * inv).T
    cos = jnp.concatenate([jnp.cos(ar), jnp.cos(ar), jnp.cos(ac), jnp.cos(ac)], axis=0)
    sin = jnp.concatenate([-jnp.sin(ar), jnp.sin(ar), -jnp.sin(ac), jnp.sin(ac)], axis=0)
    return cos.astype(F32), sin.astype(F32)


def _w_cols(w_in, name, scale=None):
    a, b = _OFF[name]
    w = w_in[..., a:b]
    return w if scale is None else w * scale


def _prepare_params(w_in, a_conv_w, a_gate_bias, c_rpb, w_up_a, w_up_b, w_up_c, w_up_d, w_out,
                    w_ffn_gate, w_ffn_up, w_ffn_down, seq):
    scales = {"cq": C_DH ** -0.5, "dq": D_DH ** -0.5 * LOG2E}
    transposed = lambda w: jnp.swapaxes(w, -1, -2).astype(BF16)
    p = {}
    p["w_main"] = jnp.concatenate([_w_cols(w_in, n, scales.get(n)) for n in _MAIN_ORDER], axis=-1).astype(BF16)
    ga, _ = _OFF["ag"]
    ngroups, ngates = A_HEADS // A_HPS, 4 * A_HPS
    gate_cols = [ty * A_HEADS + A_HPS * g + hh for g in range(ngroups) for ty in range(4) for hh in range(A_HPS)]
    wg = _w_cols(w_in, "ag")
    gb = a_gate_bias
    if gate_cols != list(range(4 * A_HEADS)):
        wg, gb = wg[..., jnp.asarray(gate_cols)], gb[..., jnp.asarray(gate_cols)]
    depth = w_in.shape[0]
    padded = jnp.pad(wg.reshape(depth, D_MODEL, ngroups, ngates), ((0, 0), (0, 0), (0, 0), (0, 128 - ngates)))
    p["w_gate"] = padded.reshape(depth, D_MODEL, ngroups * 128).astype(BF16)
    p["wt_gate"] = transposed(wg)
    gb = gb.reshape(depth, ngroups, ngates).astype(F32)
    p["gate_bias"] = jnp.pad(gb, ((0, 0), (0, 0), (0, 128 - ngates))).reshape(depth, 1, ngroups * 128)
    p["gate_bias_col"] = gb.reshape(depth, ngroups * ngates, 1)
    p["wt_av"] = transposed(_w_cols(w_in, "av"))
    p["wt_b"] = transposed(jnp.concatenate([_w_cols(w_in, n) for n in ("bq", "bk", "bv")], axis=-1))
    p["wt_d"] = transposed(jnp.concatenate([_w_cols(w_in, "dq", scales["dq"]), _w_cols(w_in, "dv")], axis=-1))
    cw = a_conv_w.astype(F32)
    gw_a = A_HPS * A_DH
    p["conv_w"] = jnp.stack([jnp.concatenate([cw[..., g * gw_a:(g + 1) * gw_a],
                                              cw[..., A_WIDTH + g * gw_a:A_WIDTH + (g + 1) * gw_a]], axis=-1)
                             for g in range(ngroups)], axis=1)
    p["natten_bias"] = jax.vmap(lambda r: _natten_bias_table(r, seq // GRID_W))(c_rpb)
    p["w_up"] = jnp.stack([w_up_a, w_up_b, w_up_c, w_up_d], axis=1).astype(BF16)
    p["w_out"] = w_out.astype(BF16)
    p["w_ffn_gate"] = w_ffn_gate.astype(BF16)
    p["w_ffn_up"] = w_ffn_up.astype(BF16)
    p["w_ffn_down"] = w_ffn_down.astype(BF16)
    return p


def _layer(x2, l, bsz, seq, cos, sin, p, norm1_g, a_norm_g, b_qnorm_g, b_knorm_g,
           d_lq1, d_lk1, d_lq2, d_lk2, d_subln_g, norm2_g, final_g, final):
    main, gates, h = _proj(x2, norm1_g.reshape(1, D_MODEL).astype(F32), p["w_main"], p["w_gate"])
    fm_d = _projt(h, p["wt_d"], BF16, "proj_fm_d")

    vt_a = _projt(h, p["wt_av"], BF16, "proj_fm_av", chunk=A_CHUNK)
    gates_t = _projt(h, p["wt_gate"], F32, "proj_gates_t", chunk=A_CHUNK)
    y_a = _mlstm(main, vt_a, gates, gates_t, p["gate_bias"], p["gate_bias_col"], p["conv_w"],
                 a_norm_g.reshape(1, A_WIDTH).astype(F32), bsz, seq)

    qt_b, k_b, v_b = _gqa_proj(h, p["wt_b"], cos, sin, b_qnorm_g.reshape(B_DH, 1).astype(F32),
                               b_knorm_g.reshape(B_DH, 1).astype(F32), seq)
    y_b = _gqa_attn(qt_b, k_b, v_b, bsz, seq)

    y_c = _natten(main, p["natten_bias"], bsz, seq)

    lambda_init = 0.8 - 0.6 * math.exp(-0.3 * l)
    vec = lambda a: a.reshape(1, D_DH).astype(F32)
    y_d = _diff_attn(fm_d, main, vec(d_lq1), vec(d_lk1), vec(d_lq2), vec(d_lk2),
                     d_subln_g.reshape(D_DV, 1).astype(F32), lambda_init, bsz, seq)

    x2 = _merge(x2, main, y_a, y_b, y_c, y_d, p["w_up"], p["w_out"])
    return _ffn(x2, norm2_g.reshape(1, D_MODEL).astype(F32), p["w_ffn_gate"], p["w_ffn_up"], p["w_ffn_down"],
                final_g.reshape(1, D_MODEL).astype(F32), final)


def kernel(x, norm1_g, w_in, a_conv_w, a_gate_bias, a_norm_g, b_qnorm_g, b_knorm_g, c_rpb, d_lambda_q1, d_lambda_k1, d_lambda_q2, d_lambda_k2, d_subln_g, w_up_a, w_up_b, w_up_c, w_up_d, w_out, norm2_g, w_ffn_gate, w_ffn_up, w_ffn_down, final_g):
    bsz, seq, _ = x.shape
    assert seq % KC == 0 and seq % GRID_W == 0
    cos, sin = _rope_tables(seq)
    params = _prepare_params(w_in, a_conv_w, a_gate_bias, c_rpb, w_up_a, w_up_b, w_up_c, w_up_d, w_out,
                             w_ffn_gate, w_ffn_up, w_ffn_down, seq)
    x2 = x.reshape(bsz * seq, D_MODEL)
    depth = norm1_g.shape[0]
    for l in range(depth):
        x2 = _layer(x2, l, bsz, seq, cos, sin, {k: v[l] for k, v in params.items()}, norm1_g[l], a_norm_g[l],
                    b_qnorm_g[l], b_knorm_g[l], d_lambda_q1[l], d_lambda_k1[l], d_lambda_q2[l],
                    d_lambda_k2[l], d_subln_g[l], norm2_g[l], final_g, l == depth - 1)
    return x2.reshape(bsz, seq, D_MODEL)
```

```python
import functools
import math

import jax
import jax.numpy as jnp
import numpy as np
from jax import lax
from jax.experimental import pallas as pl
from jax.experimental.pallas import tpu as pltpu

F32 = jnp.float32
BF16 = jnp.bfloat16

D_MODEL = 1024
DEPTH = 2
GRID_W = 64
EPS = 1e-6
N_BRANCH = 4
A_HEADS, A_DH, A_CHUNK = 4, 128, 128
A_WIDTH = A_HEADS * A_DH
B_HEADS, B_KV, B_DH = 8, 2, 64
B_WIDTH = B_HEADS * B_DH
ROPE_THETA = 10000.0
C_HEADS, C_DH, C_WIN_R, C_WIN_C = 8, 64, 8, 16
C_WIDTH = C_HEADS * C_DH
D_HEADS, D_DH = 4, 64
D_DV = 2 * D_DH
D_WIDTH = D_HEADS * D_DV
D_FF = ((8 * D_MODEL + 3 * 256 - 1) // (3 * 256)) * 256
FF_SPLITS = ((0, 1536), (1536, D_FF))

_SIZES = (A_WIDTH, A_WIDTH, A_WIDTH, A_WIDTH, 4 * A_HEADS, B_WIDTH, B_KV * B_DH, B_KV * B_DH,
          C_WIDTH, C_WIDTH, C_WIDTH, 2 * D_HEADS * D_DH, 2 * D_HEADS * D_DH, D_WIDTH, N_BRANCH * D_MODEL)
_NAMES = ("aq", "ak", "av", "ao", "ag", "bq", "bk", "bv", "cq", "ck", "cv", "dq", "dk", "dv", "gl")
_OFF = {}
_o = 0
for _n, _s in zip(_NAMES, _SIZES):
    _OFF[_n] = (_o, _o + _s)
    _o += _s

_MAIN_ORDER = ("gl", "aq", "ak", "ao", "cq", "ck", "cv", "dk")
_MAIN_OFF = {}
_o = 0
for _n in _MAIN_ORDER:
    _MAIN_OFF[_n] = _o
    _o += _OFF[_n][1] - _OFF[_n][0]
MAIN_N = _o

KC = 512
VMEM_LIMIT = 56 * 1024 * 1024


def _cparams(sem, vmem=VMEM_LIMIT):
    return pltpu.CompilerParams(dimension_semantics=sem, vmem_limit_bytes=vmem)


def _sigmoid(x):
    return 1.0 / (1.0 + jnp.exp(-x))


def _aligned(x, m):
    return x if isinstance(x, int) else pl.multiple_of(x, m)


def _proj_kernel(x_ref, g_ref, w_ref, wg_ref, o_ref, og_ref, h_ref):
    @pl.when(pl.program_id(1) == 0)
    def _():
        x = x_ref[...]
        ms = jnp.mean(x * x, axis=-1, keepdims=True)
        hb = (x * lax.rsqrt(ms + EPS) * g_ref[...]).astype(BF16)
        h_ref[...] = hb
        og_ref[...] = jnp.dot(hb, wg_ref[...], preferred_element_type=F32)

    o_ref[...] = jnp.dot(h_ref[...], w_ref[...], preferred_element_type=F32).astype(o_ref.dtype)


def _proj(x2, g, w_main, w_gate):
    t = x2.shape[0]
    tm = min(2048, t)
    n = w_main.shape[1]
    tn = 768 if n % 768 == 0 else 1024
    ng = w_gate.shape[1]
    return pl.pallas_call(
        _proj_kernel,
        out_shape=(jax.ShapeDtypeStruct((t, n), BF16),
                   jax.ShapeDtypeStruct((t, ng), F32),
                   jax.ShapeDtypeStruct((t, D_MODEL), BF16)),
        grid=(t // tm, n // tn),
        in_specs=[pl.BlockSpec((tm, D_MODEL), lambda i, j: (i, 0)),
                  pl.BlockSpec((1, D_MODEL), lambda i, j: (0, 0)),
                  pl.BlockSpec((D_MODEL, tn), lambda i, j: (0, j)),
                  pl.BlockSpec((D_MODEL, ng), lambda i, j: (0, 0))],
        out_specs=(pl.BlockSpec((tm, tn), lambda i, j: (i, j)),
                   pl.BlockSpec((tm, ng), lambda i, j: (i, 0)),
                   pl.BlockSpec((tm, D_MODEL), lambda i, j: (i, 0))),
        compiler_params=_cparams(("parallel", "arbitrary")),
        name="proj_main",
    )(x2, g, w_main, w_gate)


def _projt_kernel(h_ref, wt_ref, o_ref, *, nchunk, chunk):
    for c in range(nchunk):
        hc = h_ref[c * chunk:(c + 1) * chunk, :]
        o_ref[c] = lax.dot_general(wt_ref[...], hc, (((1,), (1,)), ((), ())),
                                   preferred_element_type=F32).astype(o_ref.dtype)


def _projt(h, wt, out_dtype, name, chunk=KC):
    t = h.shape[0]
    n = wt.shape[0]
    tm = min(2048, t)
    tn = n if n <= 768 else (768 if n % 768 == 0 else 512)
    nchunk = tm // chunk
    return pl.pallas_call(
        functools.partial(_projt_kernel, nchunk=nchunk, chunk=chunk),
        out_shape=jax.ShapeDtypeStruct((t // chunk, n, chunk), out_dtype),
        grid=(t // tm, n // tn),
        in_specs=[pl.BlockSpec((tm, D_MODEL), lambda i, j: (i, 0)),
                  pl.BlockSpec((tn, D_MODEL), lambda i, j: (j, 0))],
        out_specs=pl.BlockSpec((nchunk, tn, chunk), lambda i, j: (i, j, 0)),
        compiler_params=_cparams(("parallel", "arbitrary")),
        name=name,
    )(h, wt)


def _log_sigmoid(x):
    return jnp.minimum(x, 0.0) - jnp.log1p(jnp.exp(-jnp.abs(x)))


def _tri_dot(mat, x):
    hi = x.astype(BF16)
    r1 = x - hi.astype(F32)
    mid = r1.astype(BF16)
    lo = (r1 - mid.astype(F32)).astype(BF16)
    return (jnp.dot(mat, hi, preferred_element_type=F32)
            + jnp.dot(mat, mid, preferred_element_type=F32)
            + jnp.dot(mat, lo, preferred_element_type=F32))


def _split3(x):
    hi = x.astype(BF16)
    r1 = x - hi.astype(F32)
    mid = r1.astype(BF16)
    return hi, mid, (r1 - mid.astype(F32)).astype(BF16)


def _tri_dot_r(x, mat):
    return sum(jnp.dot(piece, mat, preferred_element_type=F32) for piece in _split3(x))


NROWS = 8
A_HPS = 4


def _mlstm_kernel(q_ref, k_ref, vt_ref, o_ref, g_ref, gt_ref, gb_ref, gbc_ref, cw_ref, ng_ref, y_ref,
                  qs_ref, ks_ref, hs_ref, st_ref, m_ref, *, seq):
    L = A_CHUNK
    nc = seq // L
    hw = A_HPS * A_DH

    rowi = lax.broadcasted_iota(jnp.int32, (L, hw), 0)

    def conv_body(c, carry):
        r0 = pl.multiple_of(c * L, L)
        pstart = pl.multiple_of(jnp.maximum(r0 - 16, 0), 16)
        nstart = pl.multiple_of(jnp.minimum(r0 + L, seq - 16), 16)
        has_prev = jnp.where(c > 0, 1.0, 0.0)
        has_next = jnp.where(c < nc - 1, 1.0, 0.0)
        for src, dst, woff, scale in ((q_ref, qs_ref, 0, 1.0), (k_ref, ks_ref, hw, A_DH ** -0.5)):
            xc = src[pl.ds(r0, L), :].astype(F32)
            prev = src[pl.ds(pstart, 16), :].astype(F32)[15:16] * has_prev
            nxt = src[pl.ds(nstart, 16), :].astype(F32)[0:1] * has_next
            xp = jnp.where(rowi == 0, prev, pltpu.roll(xc, 1, 0))
            xn = jnp.where(rowi == L - 1, nxt, pltpu.roll(xc, L - 1, 0))
            w = cw_ref[:, woff:woff + hw]
            y = xp * w[0:1] + xc * w[1:2] + xn * w[2:3]
            y = y * _sigmoid(y)
            dst[pl.ds(r0, L), :] = (y * scale).astype(BF16)
        return carry

    lax.fori_loop(0, nc, conv_body, 0)

    st_ref[...] = jnp.zeros(st_ref.shape, F32)
    m_ref[...] = jnp.zeros(m_ref.shape, F32)

    ti = lax.broadcasted_iota(jnp.int32, (L, L), 0)
    tj = lax.broadcasted_iota(jnp.int32, (L, L), 1)
    lower = tj <= ti
    upper = tj >= ti
    lmat = jnp.where(lower, 1.0, 0.0).astype(BF16)
    umat = jnp.where(upper, 1.0, 0.0).astype(BF16)
    nt = (((1,), (1,)), ((), ()))

    def step(cf, cb):
        chains = []
        for d, c in ((0, cf), (1, cb)):
            r0 = pl.multiple_of(c * L, L)
            gcol = g_ref[pl.ds(r0, L), :] + gb_ref[...]
            bcol_all = _tri_dot(lmat if d == 0 else umat, _log_sigmoid(gcol))
            grow = gt_ref[c] + gbc_ref[...]
            brow_all = _tri_dot_r(_log_sigmoid(grow), umat if d == 0 else lmat)
            for hh in range(A_HPS):
                il = 2 * A_HPS * d + hh
                fl = il + A_HPS
                ch = dict(d=d, idx=A_HPS * d + hh, c=c)
                ch["acol"] = gcol[:, il:il + 1] - bcol_all[:, fl:fl + 1]
                ch["irow"] = grow[il:il + 1, :]
                ch["brow"] = brow_all[fl:fl + 1, :]
                ch["q"] = qs_ref[pl.ds(r0, L), hh * A_DH:(hh + 1) * A_DH]
                ch["k"] = ks_ref[pl.ds(r0, L), hh * A_DH:(hh + 1) * A_DH]
                ch["vt"] = vt_ref[c, hh * A_DH:(hh + 1) * A_DH, :]
                ch["state"] = st_ref[ch["idx"]]
                ch["st"] = lax.dot_general(ch["k"], ch["q"], nt, preferred_element_type=F32)
                ch["it"] = lax.dot_general(ch["state"].astype(BF16), ch["q"], nt,
                                           preferred_element_type=F32)
                chains.append(ch)
        for ch in chains:
            d, brow = ch["d"], ch["brow"]
            mask = upper if d == 0 else lower
            m_prev = m_ref[ch["idx"]][:, 0:1]
            dmt = jnp.where(mask, brow + ch["acol"], -jnp.inf)
            inter = brow + m_prev
            mt = jnp.maximum(inter, jnp.max(dmt, axis=0, keepdims=True))
            ch["w_inter"] = jnp.exp(inter - mt)
            ch["floor"] = jnp.exp(-mt)
            sqk = ch["st"] * jnp.exp(dmt - mt)
            ch["sqk_sum"] = jnp.sum(sqk, axis=0, keepdims=True)
            ch["sqk"] = sqk.astype(BF16)
            bl = brow[:, L - 1:L] if d == 0 else brow[:, 0:1]
            gvec = bl - brow + ch["irow"]
            m_new = jnp.maximum(bl + m_prev, jnp.max(gvec, axis=1, keepdims=True))
            ch["wc"] = jnp.exp(bl + m_prev - m_new)
            ws = jnp.exp(gvec - m_new)
            ch["lhs"] = jnp.concatenate([ch["vt"].astype(F32) * ws, jnp.broadcast_to(ws, (NROWS, L))],
                                        axis=0).astype(BF16)
            m_ref[ch["idx"]] = jnp.broadcast_to(m_new, (1, 128))
        for ch in chains:
            ch["pv"] = jnp.dot(ch["vt"], ch["sqk"], preferred_element_type=F32)
            ch["upd"] = jnp.dot(ch["lhs"], ch["k"], preferred_element_type=F32)
        outs = {0: [], 1: []}
        for ch in chains:
            num = ch["w_inter"] * ch["it"][:A_DH] + ch["pv"]
            den = ch["w_inter"] * ch["it"][A_DH:A_DH + 1] + ch["sqk_sum"]
            outs[ch["d"]].append(num / jnp.maximum(jnp.abs(den), ch["floor"]))
            st_ref[ch["idx"]] = ch["wc"] * ch["state"] + ch["upd"]
        return outs

    def finalize(tots, r0):
        parts = []
        for tot in tots:
            mu = jnp.mean(tot, axis=0, keepdims=True)
            cen = tot - mu
            var = jnp.mean(cen * cen, axis=0, keepdims=True)
            parts.append((cen * lax.rsqrt(var + EPS)).T)
        hn = jnp.concatenate(parts, axis=1) * ng_ref[...]
        y_ref[pl.ds(r0, L), :] = (hn * _sigmoid(o_ref[pl.ds(r0, L), :].astype(F32))).astype(y_ref.dtype)

    def first_half(j, carry):
        outs = step(j, nc - 1 - j)
        for d, c in ((0, j), (1, nc - 1 - j)):
            for hh in range(A_HPS):
                hs_ref[c, hh] = outs[d][hh]
        return carry

    def second_half(j, carry):
        outs = step(j, nc - 1 - j)
        for d, c in ((0, j), (1, nc - 1 - j)):
            finalize([hs_ref[c, hh] + outs[d][hh] for hh in range(A_HPS)], pl.multiple_of(c * L, L))
        return carry

    lax.fori_loop(0, nc // 2, first_half, 0)
    lax.fori_loop(nc // 2, nc, second_half, 0)


def _mlstm(main, vt, gates, gates_t, gate_bias, gate_bias_col, conv_w, norm_g, bsz, seq):
    t = bsz * seq
    hw = A_HPS * A_DH
    nc = seq // A_CHUNK
    ngates = 4 * A_HPS
    assert nc % 2 == 0
    once = pl.Buffered(1)

    def col(name):
        base = _MAIN_OFF[name] // hw
        return pl.BlockSpec((seq, hw), lambda b, p: (b, base + p), pipeline_mode=once)

    return pl.pallas_call(
        functools.partial(_mlstm_kernel, seq=seq),
        out_shape=jax.ShapeDtypeStruct((t, A_WIDTH), BF16),
        grid=(bsz, A_HEADS // A_HPS),
        in_specs=[col("aq"), col("ak"),
                  pl.BlockSpec((nc, hw, A_CHUNK), lambda b, p: (b, p, 0), pipeline_mode=once),
                  col("ao"),
                  pl.BlockSpec((seq, 128), lambda b, p: (b, p), pipeline_mode=once),
                  pl.BlockSpec((nc, ngates, A_CHUNK), lambda b, p: (b, p, 0), pipeline_mode=once),
                  pl.BlockSpec((1, 128), lambda b, p: (0, p)),
                  pl.BlockSpec((ngates, 1), lambda b, p: (p, 0)),
                  pl.BlockSpec((None, 3, 2 * hw), lambda b, p: (p, 0, 0)),
                  pl.BlockSpec((1, hw), lambda b, p: (0, p))],
        out_specs=pl.BlockSpec((seq, hw), lambda b, p: (b, p)),
        scratch_shapes=[pltpu.VMEM((seq, hw), BF16), pltpu.VMEM((seq, hw), BF16),
                        pltpu.VMEM((nc, A_HPS, A_DH, A_CHUNK), F32),
                        pltpu.VMEM((2 * A_HPS, A_DH + NROWS, A_DH), F32),
                        pltpu.VMEM((2 * A_HPS, 1, 128), F32)],
        compiler_params=_cparams(("parallel", "arbitrary")),
        name="mlstm",
    )(main, main, vt, main, gates, gates_t, gate_bias, gate_bias_col, conv_w, norm_g)


LOG2E = 1.4426950408889634


def _stash_scores(s, s_ref, mx_ref, idx):
    tk, tq = s.shape
    s_ref[idx] = s
    mx_ref[idx] = jnp.max(s.reshape(tk // 8, 8, tq), axis=0)


ONES_ROWS = 8


def _with_ones(vc):
    return jnp.concatenate([vc, jnp.ones((ONES_ROWS, vc.shape[1]), vc.dtype)], axis=0)


def _flash_update(s_ref, mx_ref, tile, vc1, m_ref, acc_ref, qb, shift=None):
    m_old = m_ref[qb, tile]
    m_new = jnp.maximum(m_old, jnp.max(mx_ref[tile], axis=0, keepdims=True))
    alpha = jnp.exp2(m_old - m_new)
    p = jnp.exp2(s_ref[tile] - (m_new if shift is None else m_new - shift))
    acc_ref[qb, tile] = alpha * acc_ref[qb, tile] + jnp.dot(vc1, p.astype(BF16), preferred_element_type=F32)
    m_ref[qb, tile] = m_new


def _normalised(acc, dv):
    return acc[:dv] / acc[dv:dv + 1]


HK = KC // 2


def _pipelined_sweep(nkc, nq, scores, consume, unroll):
    total = nkc * nq

    def step(it):
        c, i = it // nq, it % nq
        nxt = it + 1
        scores(c, i, 1)
        consume(c, i, 0)
        scores(nxt // nq, nxt % nq, 0)
        consume(c, i, 1)

    scores(0, 0, 0)
    trips = (total - 1) // unroll

    def body(j, carry):
        for u in range(unroll):
            step(j * unroll + u)
        return carry

    lax.fori_loop(0, trips, body, 0)
    for it in range(trips * unroll, total - 1):
        step(it)
    scores(nkc - 1, nq - 1, 1)
    consume(nkc - 1, nq - 1, 0)
    consume(nkc - 1, nq - 1, 1)


def _gqa_proj_kernel(h_ref, wt_ref, cos_ref, sin_ref, qg_ref, kg_ref, q_ref, k_ref, v_ref, *, nchunk):
    def norm_rope(xh, g, cos, sin):
        ms = jnp.mean(xh * xh, axis=0, keepdims=True)
        xn = xh * lax.rsqrt(ms + EPS) * g
        partner = jnp.concatenate([xn[16:32], xn[0:16], xn[48:64], xn[32:48]], axis=0)
        return xn * cos + partner * sin

    for c in range(nchunk):
        tok = slice(c * KC, (c + 1) * KC)
        x = lax.dot_general(wt_ref[...], h_ref[tok, :], (((1,), (1,)), ((), ())), preferred_element_type=F32)
        cos, sin = cos_ref[:, tok], sin_ref[:, tok]
        for h in range(B_HEADS):
            qh = norm_rope(x[h * B_DH:(h + 1) * B_DH], qg_ref[...], cos, sin) * (B_DH ** -0.5 * LOG2E)
            q_ref[c, h * B_DH:(h + 1) * B_DH, :] = qh.astype(BF16)
        ks = [norm_rope(x[B_WIDTH + g * B_DH:B_WIDTH + (g + 1) * B_DH], kg_ref[...], cos, sin)
              for g in range(B_KV)]
        k_ref[tok, :] = jnp.concatenate(ks, axis=0).T.astype(BF16)
        v_ref[c] = x[B_WIDTH + B_KV * B_DH:].astype(BF16)


def _gqa_proj(h, wt_b, cos, sin, qg, kg, seq):
    t = h.shape[0]
    tm = min(2048, seq)
    nchunk = tm // KC
    spt = seq // tm
    kvw = B_KV * B_DH
    nrows = B_WIDTH + 2 * kvw
    return pl.pallas_call(
        functools.partial(_gqa_proj_kernel, nchunk=nchunk),
        out_shape=(jax.ShapeDtypeStruct((t // KC, B_WIDTH, KC), BF16),
                   jax.ShapeDtypeStruct((t, kvw), BF16),
                   jax.ShapeDtypeStruct((t // KC, kvw, KC), BF16)),
        grid=(t // tm,),
        in_specs=[pl.BlockSpec((tm, D_MODEL), lambda i: (i, 0)),
                  pl.BlockSpec((nrows, D_MODEL), lambda i: (0, 0)),
                  pl.BlockSpec((B_DH, tm), lambda i: (0, i % spt)),
                  pl.BlockSpec((B_DH, tm), lambda i: (0, i % spt)),
                  pl.BlockSpec((B_DH, 1), lambda i: (0, 0)),
                  pl.BlockSpec((B_DH, 1), lambda i: (0, 0))],
        out_specs=(pl.BlockSpec((nchunk, B_WIDTH, KC), lambda i: (i, 0, 0)),
                   pl.BlockSpec((tm, kvw), lambda i: (i, 0)),
                   pl.BlockSpec((nchunk, kvw, KC), lambda i: (i, 0, 0))),
        compiler_params=_cparams(("parallel",)),
        name="gqa_proj",
    )(h, wt_b, cos, sin, qg, kg)


def _gqa_attn_kernel(q_ref, k_ref, v_ref, o_ref, qpad_ref, s0_ref, s1_ref, mx0_ref, mx1_ref,
                     m_ref, acc_ref, *, nkc, tq):
    g = pl.program_id(1)
    rep = B_HEADS // B_KV
    nq = q_ref.shape[0]
    row = lax.broadcasted_iota(jnp.int32, (2 * B_DH, tq), 0)
    sel = (row // B_DH) == g

    def pad_queries(i, carry):
        for r in range(rep):
            qh = q_ref[i, r * B_DH:(r + 1) * B_DH, :].astype(F32)
            qpad_ref[i, r] = jnp.where(sel, jnp.concatenate([qh, qh], axis=0), 0.0).astype(BF16)
        return carry

    lax.fori_loop(0, nq, pad_queries, 0)
    m_ref[...] = jnp.full(m_ref.shape, -jnp.inf, F32)
    acc_ref[...] = jnp.zeros(acc_ref.shape, F32)
    slots = ((s0_ref, mx0_ref), (s1_ref, mx1_ref))

    def scores(c, i, half):
        s_ref, mx_ref = slots[half]
        kc = k_ref[pl.ds(_aligned(c * KC + half * HK, HK), HK), :]
        for r in range(rep):
            _stash_scores(jnp.dot(kc, qpad_ref[i, r], preferred_element_type=F32), s_ref, mx_ref, r)

    def consume(c, i, half):
        s_ref, mx_ref = slots[half]
        vc1 = _with_ones(v_ref[c, :, half * HK:(half + 1) * HK])
        for r in range(rep):
            _flash_update(s_ref, mx_ref, r, vc1, m_ref, acc_ref, i)

    _pipelined_sweep(nkc, nq, scores, consume, unroll=4)

    def write_out(i, carry):
        outs = [_normalised(acc_ref[i, r], B_DH) for r in range(rep)]
        o_ref[pl.ds(pl.multiple_of(i * tq, tq), tq), :] = jnp.concatenate(outs, axis=0).T.astype(o_ref.dtype)
        return carry

    lax.fori_loop(0, nq, write_out, 0)


def _gqa_attn(qt, k_tok, v_fm, bsz, seq):
    t = bsz * seq
    tq = KC
    nq = seq // tq
    nkc = seq // KC
    rep = B_HEADS // B_KV
    gw = rep * B_DH
    return pl.pallas_call(
        functools.partial(_gqa_attn_kernel, nkc=nkc, tq=tq),
        out_shape=jax.ShapeDtypeStruct((t, B_WIDTH), BF16),
        grid=(bsz, B_KV),
        in_specs=[pl.BlockSpec((nq, gw, tq), lambda b, g: (b, g, 0)),
                  pl.BlockSpec((seq, B_KV * B_DH), lambda b, g: (b, 0)),
                  pl.BlockSpec((nkc, B_DH, KC), lambda b, g: (b, g, 0))],
        out_specs=pl.BlockSpec((seq, gw), lambda b, g: (b, g)),
        scratch_shapes=[pltpu.VMEM((nq, rep, 2 * B_DH, tq), BF16),
                        pltpu.VMEM((rep, HK, tq), F32), pltpu.VMEM((rep, HK, tq), F32),
                        pltpu.VMEM((rep, 8, tq), F32), pltpu.VMEM((rep, 8, tq), F32),
                        pltpu.VMEM((nq, rep, 1, tq), F32),
                        pltpu.VMEM((nq, rep, B_DH + ONES_ROWS, tq), F32)],
        compiler_params=_cparams(("parallel", "arbitrary")),
        name="gqa_attn",
    )(qt, k_tok, v_fm)


NAT_ROWS = 8


def _natten_kernel(q_ref, k_ref, v_ref, *rest, rows):
    bias_refs, o_ref = rest[:NAT_ROWS], rest[NAT_ROWS]
    nkeys = C_WIN_R * GRID_W
    npair = C_HEADS // 2
    own = (lax.broadcasted_iota(jnp.int32, (2 * GRID_W, 2 * C_DH), 0) // GRID_W
           == lax.broadcasted_iota(jnp.int32, (2 * GRID_W, 2 * C_DH), 1) // C_DH)
    starts = []
    for rr in range(NAT_ROWS):
        r = pl.program_id(1) * NAT_ROWS + rr
        rs = jnp.clip(r - C_WIN_R // 2, 0, rows - C_WIN_R)
        starts.append(pl.multiple_of(rs * GRID_W, GRID_W))
    units = [(rr, p) for rr in range(NAT_ROWS) for p in range(npair)]
    scores = []
    for rr, p in units:
        kp = k_ref[pl.ds(starts[rr], nkeys), p * 128:(p + 1) * 128]
        qp = q_ref[rr * GRID_W:(rr + 1) * GRID_W, p * 128:(p + 1) * 128].astype(F32)
        qm = jnp.where(own, jnp.concatenate([qp, qp], axis=0), 0.0).astype(BF16)
        s = lax.dot_general(qm, kp, (((1,), (1,)), ((), ())), preferred_element_type=F32)
        scores.append(s + bias_refs[rr][0, 2 * p:2 * p + 2].reshape(2 * GRID_W, nkeys))
    probs = []
    for s in scores:
        e = jnp.exp(s - jnp.max(s, axis=1, keepdims=True))
        probs.append((e.astype(BF16), jnp.sum(e, axis=1, keepdims=True)))
    outs = [[] for _ in range(NAT_ROWS)]
    for (rr, p), (e, l) in zip(units, probs):
        vp = v_ref[pl.ds(starts[rr], nkeys), p * 128:(p + 1) * 128]
        o = jnp.where(own, jnp.dot(e, vp, preferred_element_type=F32) / l, 0.0)
        outs[rr].append(o[:GRID_W] + o[GRID_W:])
    o_ref[...] = jnp.concatenate([jnp.concatenate(o, axis=1) for o in outs], axis=0).astype(o_ref.dtype)


def _natten_bias_table(rpb, rows):
    c = np.arange(GRID_W)[:, None]
    kc = np.arange(GRID_W)[None, :]
    cs = np.clip(c - C_WIN_C // 2, 0, GRID_W - C_WIN_C)
    valid = (kc >= cs) & (kc < cs + C_WIN_C)
    nd = 2 * C_WIN_C - 1
    onehot = (kc - c + (C_WIN_C - 1))[None] == np.arange(nd)[:, None, None]
    toe = jnp.sum(jnp.where(onehot[None, None], rpb.astype(F32)[:, :, :, None, None], 0.0), axis=2)
    toe = jnp.where(valid[None, None], toe, -1e30)
    tbl = jnp.stack([toe[:, C_WIN_R - 1 - v:2 * C_WIN_R - 1 - v] for v in range(C_WIN_R)], axis=0)
    return jnp.swapaxes(tbl, 2, 3).reshape(C_WIN_R, C_HEADS, GRID_W, C_WIN_R * GRID_W)


def _natten(main, bias_tbl, bsz, seq):
    t = bsz * seq
    rows = seq // GRID_W
    assert rows >= C_WIN_R
    half = C_WIN_R // 2

    def col(name):
        return _MAIN_OFF[name] // C_WIDTH

    cq, ck, cv = col("cq"), col("ck"), col("cv")
    assert rows % NAT_ROWS == 0
    steps = rows // NAT_ROWS

    def bias_spec(rr):
        def variant(b, i):
            r = i * NAT_ROWS + rr
            return (r - jnp.clip(r - half, 0, rows - C_WIN_R), 0, 0, 0)
        return pl.BlockSpec((1, C_HEADS, GRID_W, C_WIN_R * GRID_W), variant)

    return pl.pallas_call(
        functools.partial(_natten_kernel, rows=rows),
        out_shape=jax.ShapeDtypeStruct((t, C_WIDTH), BF16),
        grid=(bsz, steps),
        in_specs=[pl.BlockSpec((NAT_ROWS * GRID_W, C_WIDTH), lambda b, i: (b * steps + i, cq)),
                  pl.BlockSpec((seq, C_WIDTH), lambda b, i: (b, ck)),
                  pl.BlockSpec((seq, C_WIDTH), lambda b, i: (b, cv))]
                 + [bias_spec(rr) for rr in range(NAT_ROWS)],
        out_specs=pl.BlockSpec((NAT_ROWS * GRID_W, C_WIDTH), lambda b, i: (b * steps + i, 0)),
        compiler_params=_cparams(("parallel", "arbitrary")),
        name="natten",
    )(main, main, main, *([bias_tbl] * NAT_ROWS))


def _diff_attn_kernel(slopes_ref, q1_ref, q2_ref, k1_ref, k2_ref, v_ref, lq1_ref, lk1_ref, lq2_ref, lk2_ref,
                      sg_ref, o_ref, qpad_ref, esl_ref, shift_ref, s0_ref, s1_ref, mx0_ref, mx1_ref,
                      m_ref, acc_ref, *, nkc, tq, lambda_init):
    pair = pl.program_id(1)
    nq = q1_ref.shape[0]
    row = lax.broadcasted_iota(jnp.int32, (2 * D_DH, tq), 0)

    def pad_queries(i, carry):
        for c, qr in enumerate((q1_ref, q2_ref)):
            qf = qr[i].astype(F32)
            for hh in range(2):
                qpad_ref[i, 2 * c + hh] = jnp.where((row // D_DH) == hh, qf, 0.0).astype(BF16)
        return carry

    lax.fori_loop(0, nq, pad_queries, 0)
    m_ref[...] = jnp.full(m_ref.shape, -jnp.inf, F32)
    acc_ref[...] = jnp.zeros(acc_ref.shape, F32)
    kk = lax.broadcasted_iota(jnp.int32, (KC, tq), 0)
    qq = lax.broadcasted_iota(jnp.int32, (KC, tq), 1)
    e = (kk - qq).astype(F32)
    for hh in range(2):
        slope = slopes_ref[2 * pair + hh]
        esl_ref[0, hh] = e * slope
        esl_ref[1, hh] = e * (-slope)
        esl_ref[2, hh] = jnp.abs(e) * (-slope)
    slots = ((s0_ref, mx0_ref), (s1_ref, mx1_ref))
    krefs = (k1_ref, k2_ref)

    def scores(c, i, half):
        s_ref, mx_ref = slots[half]
        kc = 2 * c + half
        r0 = _aligned(kc * KC, KC)
        off = lax.convert_element_type((kc - i) * KC, F32)
        side = jnp.where(kc > i, 1, jnp.where(kc == i, 2, 0))
        sign = jnp.where(kc > i, -1.0, 1.0)
        for hh in range(2):
            shift = sign * slopes_ref[2 * pair + hh] * off
            for comp in range(2):
                tile = 2 * comp + hh
                s = jnp.dot(krefs[comp][pl.ds(r0, KC), :], qpad_ref[i, tile], preferred_element_type=F32)
                s = s + esl_ref[side, hh]
                s_ref[tile] = s
                mx_ref[tile] = jnp.max(s.reshape(KC // 8, 8, tq), axis=0) + shift
                shift_ref[4 * half + tile] = shift

    def consume(c, i, half):
        s_ref, mx_ref = slots[half]
        for hh in range(2):
            vc1 = _with_ones(v_ref[2 * c + half, hh * D_DV:(hh + 1) * D_DV, :])
            for comp in range(2):
                tile = 2 * comp + hh
                _flash_update(s_ref, mx_ref, tile, vc1, m_ref, acc_ref, i, shift=shift_ref[4 * half + tile])

    _pipelined_sweep(nkc // 2, nq, scores, consume, unroll=2)
    lam = (jnp.exp(jnp.sum(lq1_ref[...] * lk1_ref[...], axis=1, keepdims=True))
           - jnp.exp(jnp.sum(lq2_ref[...] * lk2_ref[...], axis=1, keepdims=True)) + lambda_init)

    def write_out(i, carry):
        outs = []
        for hh in range(2):
            o = _normalised(acc_ref[i, hh], D_DV) - lam * _normalised(acc_ref[i, 2 + hh], D_DV)
            ms = jnp.mean(o * o, axis=0, keepdims=True)
            outs.append(o * lax.rsqrt(ms + EPS) * sg_ref[...] * (1.0 - lambda_init))
        o_ref[pl.ds(pl.multiple_of(i * tq, tq), tq), :] = jnp.concatenate(outs, axis=0).T.astype(o_ref.dtype)
        return carry

    lax.fori_loop(0, nq, write_out, 0)


def _diff_attn(fm_d, main, lq1, lk1, lq2, lk2, subln_g, lambda_init, bsz, seq):
    t = bsz * seq
    tq = KC
    nq = seq // tq
    nkc = seq // KC
    pw = 2 * D_DH
    vw = 2 * D_DV
    dk0 = _MAIN_OFF["dk"] // pw
    vrow0 = (2 * D_HEADS * D_DH) // vw
    slopes = jnp.asarray([LOG2E * 2.0 ** (-8.0 * (h + 1) / D_HEADS) for h in range(D_HEADS)], F32)

    def qspec(c):
        return pl.BlockSpec((nq, pw, tq), lambda b, p: (b, 2 * c + p, 0))

    def kspec(c):
        return pl.BlockSpec((seq, pw), lambda b, p: (b, dk0 + 2 * c + p))

    vec = pl.BlockSpec((1, D_DH), lambda b, p: (0, 0))
    return pl.pallas_call(
        functools.partial(_diff_attn_kernel, nkc=nkc, tq=tq, lambda_init=lambda_init),
        out_shape=jax.ShapeDtypeStruct((t, D_WIDTH), BF16),
        grid=(bsz, D_HEADS // 2),
        in_specs=[pl.BlockSpec(memory_space=pltpu.SMEM),
                  qspec(0), qspec(1), kspec(0), kspec(1),
                  pl.BlockSpec((nkc, vw, KC), lambda b, p: (b, vrow0 + p, 0)),
                  vec, vec, vec, vec,
                  pl.BlockSpec((D_DV, 1), lambda b, p: (0, 0))],
        out_specs=pl.BlockSpec((seq, vw), lambda b, p: (b, p)),
        scratch_shapes=[pltpu.VMEM((nq, 4, 2 * D_DH, tq), BF16),
                        pltpu.VMEM((3, 2, KC, tq), F32),
                        pltpu.SMEM((8,), F32),
                        pltpu.VMEM((4, KC, tq), F32), pltpu.VMEM((4, KC, tq), F32),
                        pltpu.VMEM((4, 8, tq), F32), pltpu.VMEM((4, 8, tq), F32),
                        pltpu.VMEM((nq, 4, 1, tq), F32),
                        pltpu.VMEM((nq, 4, D_DV + ONES_ROWS, tq), F32)],
        compiler_params=_cparams(("parallel", "arbitrary")),
        name="diff_attn",
    )(slopes, fm_d, fm_d, main, main, fm_d, lq1, lk1, lq2, lk2, subln_g)


def _merge_kernel(x_ref, gl_ref, ya_ref, yb_ref, yc_ref, yd_ref, wup_ref, wout_ref, o_ref):
    merged = None
    for g, y_ref in enumerate((ya_ref, yb_ref, yc_ref, yd_ref)):
        u = jnp.dot(y_ref[...], wup_ref[g], preferred_element_type=F32)
        gate = gl_ref[:, g * D_MODEL:(g + 1) * D_MODEL].astype(F32)
        term = _sigmoid(gate) * u
        merged = term if merged is None else merged + term
    o_ref[...] = x_ref[...] + jnp.dot(merged.astype(BF16), wout_ref[...], preferred_element_type=F32)


def _merge_ffn_kernel(x_ref, gl_ref, ya_ref, yb_ref, yc_ref, yd_ref, wup_ref, wout_ref,
                      g_ref, wg_ref, wu_ref, wd_ref, fg_ref, o_ref, x1_ref, *, final):
    _merge_kernel(x_ref, gl_ref, ya_ref, yb_ref, yc_ref, yd_ref, wup_ref, wout_ref, x1_ref)
    _ffn_kernel(x1_ref, g_ref, wg_ref, wu_ref, wd_ref, fg_ref, o_ref, final=final)


def _merge_ffn(x2, main, ya, yb, yc, yd, w_up, w_out, g, wg, wu, wd, fg, final):
    t = x2.shape[0]
    tm = min(512, t)
    row = lambda w: pl.BlockSpec((tm, w), lambda i: (i, 0))
    fixed = lambda shape: pl.BlockSpec(shape, lambda i: (0,) * len(shape), pipeline_mode=pl.Buffered(1))
    return pl.pallas_call(
        functools.partial(_merge_ffn_kernel, final=final),
        out_shape=jax.ShapeDtypeStruct((t, D_MODEL), F32),
        grid=(t // tm,),
        in_specs=[row(D_MODEL), row(N_BRANCH * D_MODEL), row(512), row(512), row(512), row(512),
                  fixed((N_BRANCH, 512, D_MODEL)), fixed((D_MODEL, D_MODEL)),
                  fixed((1, D_MODEL)), fixed((D_MODEL, D_FF)), fixed((D_MODEL, D_FF)), fixed((D_FF, D_MODEL)),
                  fixed((1, D_MODEL))],
        out_specs=row(D_MODEL),
        scratch_shapes=[pltpu.VMEM((tm, D_MODEL), F32)],
        compiler_params=_cparams(("parallel",)),
        name="merge_ffn",
    )(x2, main, ya, yb, yc, yd, w_up, w_out, g, wg, wu, wd, fg)


def _ffn_kernel(x_ref, g_ref, wg_ref, wu_ref, wd_ref, fg_ref, o_ref, *, final):
    x = x_ref[...]
    ms = jnp.mean(x * x, axis=-1, keepdims=True)
    h = (x * lax.rsqrt(ms + EPS) * g_ref[...]).astype(BF16)
    acc = x
    for a, b in FF_SPLITS:
        gt = jnp.dot(h, wg_ref[:, a:b], preferred_element_type=F32)
        up = jnp.dot(h, wu_ref[:, a:b], preferred_element_type=F32)
        act = (gt * _sigmoid(gt)) * up
        acc = acc + jnp.dot(act.astype(BF16), wd_ref[a:b, :], preferred_element_type=F32)
    if final:
        ms2 = jnp.mean(acc * acc, axis=-1, keepdims=True)
        acc = acc * lax.rsqrt(ms2 + EPS) * fg_ref[...]
    o_ref[...] = acc


def _rope_tables(seq):
    tpos = jnp.arange(seq)
    row = (tpos // GRID_W).astype(F32)
    colp = (tpos % GRID_W).astype(F32)
    n_freq = B_DH // 4
    inv = ROPE_THETA ** (-jnp.arange(n_freq, dtype=F32) / n_freq)
    ar = (row[:, None] * inv).T
    ac = (colp[:, None] * inv).T
    cos = jnp.concatenate([jnp.cos(ar), jnp.cos(ar), jnp.cos(ac), jnp.cos(ac)], axis=0)
    sin = jnp.concatenate([-jnp.sin(ar), jnp.sin(ar), -jnp.sin(ac), jnp.sin(ac)], axis=0)
    return cos.astype(F32), sin.astype(F32)


def _w_cols(w_in, name, scale=None):
    a, b = _OFF[name]
    w = w_in[..., a:b]
    return w if scale is None else w * scale


def _prepare_params(w_in, a_conv_w, a_gate_bias, c_rpb, w_up_a, w_up_b, w_up_c, w_up_d, w_out,
                    w_ffn_gate, w_ffn_up, w_ffn_down, seq):
    scales = {"cq": C_DH ** -0.5, "dq": D_DH ** -0.5 * LOG2E}
    transposed = lambda w: jnp.swapaxes(w, -1, -2).astype(BF16)
    p = {}
    p["w_main"] = jnp.concatenate([_w_cols(w_in, n, scales.get(n)) for n in _MAIN_ORDER], axis=-1).astype(BF16)
    ga, _ = _OFF["ag"]
    ngroups, ngates = A_HEADS // A_HPS, 4 * A_HPS
    gate_cols = [ty * A_HEADS + A_HPS * g + hh for g in range(ngroups) for ty in range(4) for hh in range(A_HPS)]
    wg = _w_cols(w_in, "ag")
    gb = a_gate_bias
    if gate_cols != list(range(4 * A_HEADS)):
        wg, gb = wg[..., jnp.asarray(gate_cols)], gb[..., jnp.asarray(gate_cols)]
    depth = w_in.shape[0]
    padded = jnp.pad(wg.reshape(depth, D_MODEL, ngroups, ngates), ((0, 0), (0, 0), (0, 0), (0, 128 - ngates)))
    p["w_gate"] = padded.reshape(depth, D_MODEL, ngroups * 128).astype(BF16)
    p["wt_gate"] = transposed(wg)
    gb = gb.reshape(depth, ngroups, ngates).astype(F32)
    p["gate_bias"] = jnp.pad(gb, ((0, 0), (0, 0), (0, 128 - ngates))).reshape(depth, 1, ngroups * 128)
    p["gate_bias_col"] = gb.reshape(depth, ngroups * ngates, 1)
    p["wt_av"] = transposed(_w_cols(w_in, "av"))
    p["wt_b"] = transposed(jnp.concatenate([_w_cols(w_in, n) for n in ("bq", "bk", "bv")], axis=-1))
    p["wt_d"] = transposed(jnp.concatenate([_w_cols(w_in, "dq", scales["dq"]), _w_cols(w_in, "dv")], axis=-1))
    cw = a_conv_w.astype(F32)
    gw_a = A_HPS * A_DH
    p["conv_w"] = jnp.stack([jnp.concatenate([cw[..., g * gw_a:(g + 1) * gw_a],
                                              cw[..., A_WIDTH + g * gw_a:A_WIDTH + (g + 1) * gw_a]], axis=-1)
                             for g in range(ngroups)], axis=1)
    p["natten_bias"] = jax.vmap(lambda r: _natten_bias_table(r, seq // GRID_W))(c_rpb)
    p["w_up"] = jnp.stack([w_up_a, w_up_b, w_up_c, w_up_d], axis=1).astype(BF16)
    p["w_out"] = w_out.astype(BF16)
    p["w_ffn_gate"] = w_ffn_gate.astype(BF16)
    p["w_ffn_up"] = w_ffn_up.astype(BF16)
    p["w_ffn_down"] = w_ffn_down.astype(BF16)
    return p


def _layer(x2, l, bsz, seq, cos, sin, p, norm1_g, a_norm_g, b_qnorm_g, b_knorm_g,
           d_lq1, d_lk1, d_lq2, d_lk2, d_subln_g, norm2_g, final_g, final):
    main, gates, h = _proj(x2, norm1_g.reshape(1, D_MODEL).astype(F32), p["w_main"], p["w_gate"])
    fm_d = _projt(h, p["wt_d"], BF16, "proj_fm_d")

    vt_a = _projt(h, p["wt_av"], BF16, "proj_fm_av", chunk=A_CHUNK)
    gates_t = _projt(h, p["wt_gate"], F32, "proj_gates_t", chunk=A_CHUNK)
    y_a = _mlstm(main, vt_a, gates, gates_t, p["gate_bias"], p["gate_bias_col"], p["conv_w"],
                 a_norm_g.reshape(1, A_WIDTH).astype(F32), bsz, seq)

    qt_b, k_b, v_b = _gqa_proj(h, p["wt_b"], cos, sin, b_qnorm_g.reshape(B_DH, 1).astype(F32),
                               b_knorm_g.reshape(B_DH, 1).astype(F32), seq)
    y_b = _gqa_attn(qt_b, k_b, v_b, bsz, seq)

    y_c = _natten(main, p["natten_bias"], bsz, seq)

    lambda_init = 0.8 - 0.6 * math.exp(-0.3 * l)
    vec = lambda a: a.reshape(1, D_DH).astype(F32)
    y_d = _diff_attn(fm_d, main, vec(d_lq1), vec(d_lk1), vec(d_lq2), vec(d_lk2),
                     d_subln_g.reshape(D_DV, 1).astype(F32), lambda_init, bsz, seq)

    return _merge_ffn(x2, main, y_a, y_b, y_c, y_d, p["w_up"], p["w_out"],
                      norm2_g.reshape(1, D_MODEL).astype(F32), p["w_ffn_gate"], p["w_ffn_up"], p["w_ffn_down"],
                      final_g.reshape(1, D_MODEL).astype(F32), final)


def kernel(x, norm1_g, w_in, a_conv_w, a_gate_bias, a_norm_g, b_qnorm_g, b_knorm_g, c_rpb, d_lambda_q1, d_lambda_k1, d_lambda_q2, d_lambda_k2, d_subln_g, w_up_a, w_up_b, w_up_c, w_up_d, w_out, norm2_g, w_ffn_gate, w_ffn_up, w_ffn_down, final_g):
    bsz, seq, _ = x.shape
    assert seq % KC == 0 and seq % GRID_W == 0
    cos, sin = _rope_tables(seq)
    params = _prepare_params(w_in, a_conv_w, a_gate_bias, c_rpb, w_up_a, w_up_b, w_up_c, w_up_d, w_out,
                             w_ffn_gate, w_ffn_up, w_ffn_down, seq)
    x2 = x.reshape(bsz * seq, D_MODEL)
    depth = norm1_g.shape[0]
    for l in range(depth):
        x2 = _layer(x2, l, bsz, seq, cos, sin, {k: v[l] for k, v in params.items()}, norm1_g[l], a_norm_g[l],
                    b_qnorm_g[l], b_knorm_g[l], d_lambda_q1[l], d_lambda_k1[l], d_lambda_q2[l],
                    d_lambda_k2[l], d_subln_g[l], norm2_g[l], final_g, l == depth - 1)
    return x2.reshape(bsz, seq, D_MODEL)
```
